```python
import math
import jax, jax.numpy as jnp
from jax import lax
import numpy as np

D_MODEL = 1024
BATCH = 8
SEQ = 4096
DEPTH = 1

HEAD_DIM = 64
SB_HEADS = 8
FOX_HEADS = 8
SB_WIDTH = SB_HEADS * HEAD_DIM
FOX_WIDTH = FOX_HEADS * HEAD_DIM
MIX_WIDTH = SB_WIDTH + FOX_WIDTH
IN_COLS = 3 * SB_WIDTH + 3 * FOX_WIDTH + FOX_HEADS
Q_BLOCK = 128
PEER_HEADS = 8
N_KEYS = 128
N_EXPERTS = N_KEYS * N_KEYS
PEER_TOPK = 16
PEER_DKEY = 256
PEER_HALF = PEER_DKEY // 2
PEER_CHUNK = 128
N_MOD = 6
EPS = 1e-6

kernel_name = "hybrid_sb_fox_peer_adaln_block"


def rmsnorm(x, g):
    xf = x.astype(jnp.float32)
    y = xf * lax.rsqrt(jnp.mean(xf * xf, axis=-1, keepdims=True) + EPS)
    return (y * g.astype(jnp.float32)).astype(x.dtype)


def modulate(h, shift, scale):
    return h * (1.0 + scale[:, None, :]) + shift[:, None, :]


def split_heads(t, n_heads):
    b, s, _ = t.shape
    return t.reshape(b, s, n_heads, HEAD_DIM).transpose(0, 2, 1, 3)


def headwise_rmsnorm(o, g):
    of = o.astype(jnp.float32)
    y = of * lax.rsqrt(jnp.mean(of * of, axis=-1, keepdims=True) + EPS)
    return (y * g.astype(jnp.float32)[None, :, None, :]).astype(o.dtype)


def stick_breaking_attention(q, k, v):
    s_len = q.shape[2]
    inv = 1.0 / math.sqrt(HEAD_DIM)
    outs = []
    for i0 in range(0, s_len, Q_BLOCK):
        end = i0 + Q_BLOCK
        qb = q[:, :, i0:end]
        kb = k[:, :, :end]
        vb = v[:, :, :end]
        z = jnp.einsum('bhqd,bhkd->bhqk', qb, kb).astype(jnp.float32) * inv
        qpos = i0 + jnp.arange(Q_BLOCK)
        kpos = jnp.arange(end)
        strict = kpos[None, :] < qpos[:, None]
        log1m = jnp.where(strict, -jax.nn.softplus(z), 0.0)
        after = lax.cumsum(log1m, axis=3, reverse=True) - log1m
        a = jnp.where(strict, jnp.exp(jax.nn.log_sigmoid(z) + after), 0.0)
        outs.append(jnp.einsum('bhqk,bhkd->bhqd', a.astype(vb.dtype), vb))
    return jnp.concatenate(outs, axis=2)


def forgetting_attention(q, k, v, log_f):
    s_len = q.shape[2]
    inv = 1.0 / math.sqrt(HEAD_DIM)
    cum = lax.cumsum(log_f, axis=2)
    outs = []
    for i0 in range(0, s_len, Q_BLOCK):
        end = i0 + Q_BLOCK
        qb = q[:, :, i0:end]
        kb = k[:, :, :end]
        vb = v[:, :, :end]
        z = jnp.einsum('bhqd,bhkd->bhqk', qb, kb).astype(jnp.float32) * inv
        z = z + cum[:, :, i0:end, None] - cum[:, :, None, :end]
        qpos = i0 + jnp.arange(Q_BLOCK)
        kpos = jnp.arange(end)
        causal = kpos[None, :] <= qpos[:, None]
        p = jax.nn.softmax(jnp.where(causal, z, -jnp.inf), axis=-1)
        outs.append(jnp.einsum('bhqk,bhkd->bhqd', p.astype(vb.dtype), vb))
    return jnp.concatenate(outs, axis=2)


def peer_ffn(h, w_query, sub_keys, expert_u, expert_v):
    b, s, d = h.shape
    t = b * s
    hf = h.reshape(t, d)
    q = (hf @ w_query).reshape(t, PEER_HEADS, 2, PEER_HALF)
    scores = jnp.einsum('thpc,hpnc->thpn', q, sub_keys).astype(jnp.float32)
    top_s, top_i = lax.top_k(scores, PEER_TOPK)
    cand = top_s[:, :, 0, :, None] + top_s[:, :, 1, None, :]
    best_s, best_pos = lax.top_k(cand.reshape(t, PEER_HEADS, PEER_TOPK * PEER_TOPK), PEER_TOPK)
    i1 = jnp.take_along_axis(top_i[:, :, 0], best_pos // PEER_TOPK, axis=-1)
    i2 = jnp.take_along_axis(top_i[:, :, 1], best_pos % PEER_TOPK, axis=-1)
    expert_idx = i1 * N_KEYS + i2
    gates = jax.nn.softmax(best_s, axis=-1)
    n_chunks = t // PEER_CHUNK

    def chunk_fn(args):
        xc, idx, g = args
        u = expert_u[idx]
        act = jax.nn.gelu(jnp.einsum('cd,chkd->chk', xc, u), approximate=False)
        vv = expert_v[idx]
        return jnp.einsum('chk,chkd->cd', (g * act).astype(vv.dtype), vv)

    out = lax.map(chunk_fn, (hf.reshape(n_chunks, PEER_CHUNK, d),
                             expert_idx.reshape(n_chunks, PEER_CHUNK, PEER_HEADS, PEER_TOPK),
                             gates.reshape(n_chunks, PEER_CHUNK, PEER_HEADS, PEER_TOPK)))
    return out.reshape(b, s, d).astype(h.dtype)


def setup_inputs(seed: int = 0) -> dict:
    key = jax.random.key(seed)
    ks = jax.random.split(key, 20)
    f32 = jnp.float32
    D = D_MODEL
    nrm = lambda k, shape, scale: jax.random.normal(k, shape, f32) * scale
    return {
        "x": nrm(ks[0], (BATCH, SEQ, D), 1.0),
        "c": nrm(ks[1], (BATCH, D), 1.0),
        "w_ada": nrm(ks[2], (DEPTH, D, N_MOD * D), D ** -0.5),
        "b_ada": nrm(ks[3], (DEPTH, N_MOD * D), 0.02),
        "norm_mix_g": 1.0 + nrm(ks[4], (DEPTH, D), 0.02),
        "w_in": nrm(ks[5], (DEPTH, D, IN_COLS), D ** -0.5),
        "b_fgate": 3.0 + nrm(ks[6], (DEPTH, FOX_HEADS), 0.5),
        "gn_sb_g": 1.0 + nrm(ks[7], (DEPTH, SB_HEADS, HEAD_DIM), 0.02),
        "gn_fox_g": 1.0 + nrm(ks[8], (DEPTH, FOX_HEADS, HEAD_DIM), 0.02),
        "w_out": nrm(ks[9], (DEPTH, MIX_WIDTH, D), MIX_WIDTH ** -0.5),
        "norm_ffn_g": 1.0 + nrm(ks[10], (DEPTH, D), 0.02),
        "w_query": nrm(ks[11], (DEPTH, D, PEER_HEADS * PEER_DKEY), D ** -0.5),
        "sub_keys": nrm(ks[12], (DEPTH, PEER_HEADS, 2, N_KEYS, PEER_HALF), PEER_HALF ** -0.5),
        "expert_u": nrm(ks[13], (DEPTH, N_EXPERTS, D), D ** -0.5),
        "expert_v": nrm(ks[14], (DEPTH, N_EXPERTS, D), PEER_HEADS ** -0.5),
        "w_ada_final": nrm(ks[15], (D, 2 * D), D ** -0.5),
        "b_ada_final": nrm(ks[16], (2 * D,), 0.02),
        "norm_final_g": 1.0 + nrm(ks[17], (D,), 0.02),
    }


def reference(x, c, w_ada, b_ada, norm_mix_g, w_in, b_fgate, gn_sb_g, gn_fox_g, w_out,
              norm_ffn_g, w_query, sub_keys, expert_u, expert_v, w_ada_final, b_ada_final,
              norm_final_g):
    c_act = jax.nn.silu(c)
    for l in range(DEPTH):
        mod = c_act @ w_ada[l] + b_ada[l]
        sh1, sc1, g1, sh2, sc2, g2 = jnp.split(mod, N_MOD, axis=-1)

        h = modulate(rmsnorm(x, norm_mix_g[l]), sh1, sc1)
        proj = h @ w_in[l]
        o = 0
        q_sb = proj[..., o:o + SB_WIDTH]; o += SB_WIDTH
        k_sb = proj[..., o:o + SB_WIDTH]; o += SB_WIDTH
        v_sb = proj[..., o:o + SB_WIDTH]; o += SB_WIDTH
        q_fx = proj[..., o:o + FOX_WIDTH]; o += FOX_WIDTH
        k_fx = proj[..., o:o + FOX_WIDTH]; o += FOX_WIDTH
        v_fx = proj[..., o:o + FOX_WIDTH]; o += FOX_WIDTH
        f_logit = proj[..., o:o + FOX_HEADS]

        o_sb = stick_breaking_attention(split_heads(q_sb, SB_HEADS), split_heads(k_sb, SB_HEADS),
                                        split_heads(v_sb, SB_HEADS))
        log_f = jax.nn.log_sigmoid((f_logit + b_fgate[l]).astype(jnp.float32)).transpose(0, 2, 1)
        o_fx = forgetting_attention(split_heads(q_fx, FOX_HEADS), split_heads(k_fx, FOX_HEADS),
                                    split_heads(v_fx, FOX_HEADS), log_f)
        o_sb = headwise_rmsnorm(o_sb, gn_sb_g[l])
        o_fx = headwise_rmsnorm(o_fx, gn_fox_g[l])
        b, _, s, _ = o_sb.shape
        merged = jnp.concatenate([o_sb.transpose(0, 2, 1, 3).reshape(b, s, SB_WIDTH),
                                  o_fx.transpose(0, 2, 1, 3).reshape(b, s, FOX_WIDTH)], axis=-1)
        x = x + g1[:, None, :] * (merged @ w_out[l])

        h2 = modulate(rmsnorm(x, norm_ffn_g[l]), sh2, sc2)
        x = x + g2[:, None, :] * peer_ffn(h2, w_query[l], sub_keys[l], expert_u[l], expert_v[l])

    modf = c_act @ w_ada_final + b_ada_final
    shf, scf = jnp.split(modf, 2, axis=-1)
    return modulate(rmsnorm(x, norm_final_g), shf, scf)
```

```python
import functools
import math

import jax
import jax.numpy as jnp
from jax import lax
from jax.experimental import pallas as pl
from jax.experimental.pallas import tpu as pltpu

F32 = jnp.float32
BF16 = jnp.bfloat16

D_MODEL = 1024
HEAD_DIM = 64
N_PAIRS = 4
LANES = 128
QKV_COLS = 3072
N_FGATE = 8
N_MOD = 6
EPS = 1e-6
PEER_HEADS = 8
N_KEYS = 128
PEER_TOPK = 16
N_EXPERTS = N_KEYS * N_KEYS
INV_SQRT_HD = 1.0 / math.sqrt(HEAD_DIM)
INV_SQRT2 = 1.0 / math.sqrt(2.0)
NEG_INF = float("-inf")

VMEM_LIMIT = 56 * 1024 * 1024

NT_DIMS = (((1,), (1,)), ((), ()))
TN_DIMS = (((0,), (0,)), ((), ()))

CAND = [(r1, r2) for r1 in range(PEER_TOPK) for r2 in range(PEER_TOPK)
        if (r1 + 1) * (r2 + 1) <= PEER_TOPK]


def _cparams(sem):
    return pltpu.CompilerParams(dimension_semantics=sem, vmem_limit_bytes=VMEM_LIMIT)


def _softplus(z):
    return jnp.maximum(z, 0.0) + jnp.log(1.0 + jnp.exp(-jnp.abs(z)))


def _split3(x):
    hi = x.astype(BF16)
    r1 = x - hi.astype(F32)
    mid = r1.astype(BF16)
    lo = (r1 - mid.astype(F32)).astype(BF16)
    return hi, mid, lo


def _ada_kernel(c_ref, w_ref, b_ref, o_ref):
    c = c_ref[...]
    ca = c * jax.nn.sigmoid(c)
    o_ref[...] = jnp.dot(ca, w_ref[...], preferred_element_type=F32) + b_ref[...]


def _ada(c, w, b, tn=1024):
    bsz, d = c.shape
    n = w.shape[1]
    return pl.pallas_call(
        _ada_kernel,
        grid=(n // tn,),
        in_specs=[pl.BlockSpec((bsz, d), lambda j: (0, 0)),
                  pl.BlockSpec((d, tn), lambda j: (0, j)),
                  pl.BlockSpec((1, tn), lambda j: (0, j))],
        out_specs=pl.BlockSpec((bsz, tn), lambda j: (0, j)),
        out_shape=jax.ShapeDtypeStruct((bsz, n), F32),
        compiler_params=_cparams(("arbitrary",)),
        name="ada",
    )(c, w, b.reshape(1, n))


def _inproj_kernel(x_ref, mod_ref, g_ref, w_ref, wf_ref, bf_ref, tri_ref,
                   qkv_ref, cumc_ref, cumr_ref, carry_ref):
    s = pl.program_id(1)

    @pl.when(s == 0)
    def _():
        carry_ref[...] = jnp.zeros_like(carry_ref)

    x = x_ref[0]
    sh = mod_ref[0, 0:1, :]
    sc = mod_ref[0, 1:2, :]
    ms = jnp.mean(x * x, axis=-1, keepdims=True)
    y = x * lax.rsqrt(ms + EPS) * g_ref[...]
    hb = (y * (1.0 + sc) + sh).astype(BF16)
    qkv_ref[0] = jnp.dot(hb, w_ref[...], preferred_element_type=F32).astype(BF16)
    fl = jnp.dot(hb, wf_ref[...], preferred_element_type=F32) + bf_ref[...]
    lf = -_softplus(-fl)
    tri = tri_ref[...]
    hi, mid, lo = _split3(lf)
    cum = (jnp.dot(tri, hi, preferred_element_type=F32)
           + jnp.dot(tri, mid, preferred_element_type=F32)
           + jnp.dot(tri, lo, preferred_element_type=F32)) + carry_ref[...]
    tm = cum.shape[0]
    carry_ref[...] = cum[tm - 1:tm, :]
    cumc_ref[0] = cum
    cumr_ref[0] = cum.T[0:N_FGATE, :]


def _inproj(x, mod3, g, w_qkv, w_f, b_f, tm=512):
    bsz, seq, d = x.shape
    ii = lax.broadcasted_iota(jnp.int32, (tm, tm), 0)
    jj = lax.broadcasted_iota(jnp.int32, (tm, tm), 1)
    tri = (jj <= ii).astype(BF16)
    return pl.pallas_call(
        _inproj_kernel,
        grid=(bsz, seq // tm),
        in_specs=[pl.BlockSpec((1, tm, d), lambda b, s: (b, s, 0)),
                  pl.BlockSpec((1, N_MOD, d), lambda b, s: (b, 0, 0)),
                  pl.BlockSpec((1, d), lambda b, s: (0, 0)),
                  pl.BlockSpec((d, QKV_COLS), lambda b, s: (0, 0)),
                  pl.BlockSpec((d, LANES), lambda b, s: (0, 0)),
                  pl.BlockSpec((1, LANES), lambda b, s: (0, 0)),
                  pl.BlockSpec((tm, tm), lambda b, s: (0, 0))],
        out_specs=[pl.BlockSpec((1, tm, QKV_COLS), lambda b, s: (b, s, 0)),
                   pl.BlockSpec((1, tm, LANES), lambda b, s: (b, s, 0)),
                   pl.BlockSpec((1, N_FGATE, tm), lambda b, s: (b, 0, s))],
        out_shape=[jax.ShapeDtypeStruct((bsz, seq, QKV_COLS), BF16),
                   jax.ShapeDtypeStruct((bsz, seq, LANES), F32),
                   jax.ShapeDtypeStruct((bsz, N_FGATE, seq), F32)],
        scratch_shapes=[pltpu.VMEM((1, LANES), F32)],
        compiler_params=_cparams(("parallel", "arbitrary")),
        name="inproj",
    )(x, mod3, g, w_qkv, w_f, b_f, tri)


def _head_rmsnorm_store(o_ref, acc_a, acc_b, gn_ref, lane_lo):
    o = jnp.where(lane_lo, acc_a, acc_b)
    sq = o * o
    ss_a = jnp.sum(jnp.where(lane_lo, sq, 0.0), axis=-1, keepdims=True)
    ss_b = jnp.sum(jnp.where(lane_lo, 0.0, sq), axis=-1, keepdims=True)
    ms = jnp.where(lane_lo, ss_a, ss_b) * (1.0 / HEAD_DIM)
    o_ref[0] = (o * lax.rsqrt(ms + EPS) * gn_ref[0]).astype(o_ref.dtype)


def _sb_kernel(q_ref, k_ref, v_ref, gn_ref, m_ref, o_ref, *, tq):
    i = pl.program_id(2)
    lane_lo = lax.broadcasted_iota(jnp.int32, (1, LANES), 1) < HEAD_DIM
    q2 = q_ref[0]
    zero = jnp.zeros_like(q2)
    qa = jnp.where(lane_lo, q2, zero)
    qb = jnp.where(lane_lo, zero, q2)
    msum = m_ref[...]
    row = lax.broadcasted_iota(jnp.int32, (tq, tq), 0)
    col = lax.broadcasted_iota(jnp.int32, (tq, tq), 1)
    strict = col < row

    def block(qh, kblk, vblk, acc, carry, diag):
        z = lax.dot_general(qh, kblk, NT_DIMS, preferred_element_type=F32) * INV_SQRT_HD
        sp = _softplus(z)
        l1m = -sp
        if diag:
            l1m = jnp.where(strict, l1m, 0.0)
        hi = l1m.astype(BF16)
        lo = (l1m - hi.astype(F32)).astype(BF16)
        aft = (jnp.dot(hi, msum, preferred_element_type=F32)
               + jnp.dot(lo, msum, preferred_element_type=F32))
        w = jnp.exp((z - sp) + aft + carry)
        if diag:
            w = jnp.where(strict, w, 0.0)
        acc = acc + jnp.dot(w.astype(BF16), vblk, preferred_element_type=F32)
        carry = carry + jnp.sum(l1m, axis=-1, keepdims=True)
        return acc, carry

    def step(kb, vb, state, diag):
        acc_a, car_a, acc_b, car_b = state
        acc_a, car_a = block(qa, kb, vb, acc_a, car_a, diag)
        acc_b, car_b = block(qb, kb, vb, acc_b, car_b, diag)
        return acc_a, car_a, acc_b, car_b

    zacc = jnp.zeros((tq, LANES), F32)
    zcar = jnp.zeros((tq, 1), F32)
    d0 = pl.multiple_of(i * tq, tq)
    state = step(k_ref[0, pl.ds(d0, tq), :], v_ref[0, pl.ds(d0, tq), :],
                 (zacc, zcar, zacc, zcar), True)

    def body(n, st):
        start = pl.multiple_of((i - 1 - n) * tq, tq)
        return step(k_ref[0, pl.ds(start, tq), :], v_ref[0, pl.ds(start, tq), :], st, False)

    acc_a, _, acc_b, _ = lax.fori_loop(0, i, body, state)
    _head_rmsnorm_store(o_ref, acc_a, acc_b, gn_ref, lane_lo)


def _fox_kernel(q_ref, k_ref, v_ref, cc_ref, cr_ref, gn_ref, o_ref, *, tq):
    j = pl.program_id(1)
    i = pl.program_id(2)
    lane_lo = lax.broadcasted_iota(jnp.int32, (1, LANES), 1) < HEAD_DIM
    q2 = q_ref[0]
    zero = jnp.zeros_like(q2)
    qa = jnp.where(lane_lo, q2, zero)
    qb = jnp.where(lane_lo, zero, q2)
    row = lax.broadcasted_iota(jnp.int32, (tq, tq), 0)
    col = lax.broadcasted_iota(jnp.int32, (tq, tq), 1)
    causal = col <= row
    cc = cc_ref[0]
    lane = lax.broadcasted_iota(jnp.int32, (1, LANES), 1)
    cq_a = jnp.sum(jnp.where(lane == 2 * j, cc, 0.0), axis=-1, keepdims=True)
    cq_b = jnp.sum(jnp.where(lane == 2 * j + 1, cc, 0.0), axis=-1, keepdims=True)

    def block(qh, kblk, vblk, cq, ck, st, diag):
        m, l, acc = st
        z = lax.dot_general(qh, kblk, NT_DIMS, preferred_element_type=F32) * INV_SQRT_HD
        z = z + cq - ck
        if diag:
            z = jnp.where(causal, z, NEG_INF)
        m_new = jnp.maximum(m, jnp.max(z, axis=-1, keepdims=True))
        alpha = jnp.exp(m - m_new)
        p = jnp.exp(z - m_new)
        l = alpha * l + jnp.sum(p, axis=-1, keepdims=True)
        acc = alpha * acc + jnp.dot(p.astype(BF16), vblk, preferred_element_type=F32)
        return m_new, l, acc

    def step(blk, state, diag):
        start = pl.multiple_of(blk * tq, tq)
        kb = k_ref[0, pl.ds(start, tq), :]
        vb = v_ref[0, pl.ds(start, tq), :]
        ck2 = cr_ref[0, blk]
        ck_a = jnp.sum(jnp.where(lax.broadcasted_iota(jnp.int32, ck2.shape, 0) == 2 * j, ck2, 0.0),
                       axis=0, keepdims=True)
        ck_b = jnp.sum(jnp.where(lax.broadcasted_iota(jnp.int32, ck2.shape, 0) == 2 * j + 1, ck2, 0.0),
                       axis=0, keepdims=True)
        st_a, st_b = state
        return (block(qa, kb, vb, cq_a, ck_a, st_a, diag),
                block(qb, kb, vb, cq_b, ck_b, st_b, diag))

    init = (jnp.full((tq, 1), NEG_INF, F32), jnp.zeros((tq, 1), F32), jnp.zeros((tq, LANES), F32))
    state = step(i, (init, init), True)

    def body(n, st):
        return step(n, st, False)

    (_, l_a, acc_a), (_, l_b, acc_b) = lax.fori_loop(0, i, body, state)
    _head_rmsnorm_store(o_ref, acc_a / l_a, acc_b / l_b, gn_ref, lane_lo)


def _attention(qkv, cumc, cumr, gn_sb, gn_fox, tq=256):
    bsz, seq, _ = qkv.shape
    nq = seq // tq
    ii = lax.broadcasted_iota(jnp.int32, (tq, tq), 0)
    jj = lax.broadcasted_iota(jnp.int32, (tq, tq), 1)
    msum = (ii > jj).astype(BF16)
    grid = (bsz, N_PAIRS, nq)
    sem = _cparams(("parallel", "parallel", "arbitrary"))

    def qspec(off):
        return pl.BlockSpec((1, tq, LANES), lambda b, j, i: (b, i, off + j))

    def kvspec(off):
        return pl.BlockSpec((1, seq, LANES), lambda b, j, i: (b, 0, off + j))

    gnspec = pl.BlockSpec((1, 1, LANES), lambda b, j, i: (j, 0, 0))
    out_shape = jax.ShapeDtypeStruct((bsz, seq, N_PAIRS * LANES), BF16)
    ospec = pl.BlockSpec((1, tq, LANES), lambda b, j, i: (b, i, j))

    o_sb = pl.pallas_call(
        functools.partial(_sb_kernel, tq=tq),
        grid=grid,
        in_specs=[qspec(0), kvspec(4), kvspec(8), gnspec,
                  pl.BlockSpec((tq, tq), lambda b, j, i: (0, 0))],
        out_specs=ospec, out_shape=out_shape, compiler_params=sem, name="sb_attn",
    )(qkv, qkv, qkv, gn_sb.reshape(N_PAIRS, 1, LANES), msum)

    o_fx = pl.pallas_call(
        functools.partial(_fox_kernel, tq=tq),
        grid=grid,
        in_specs=[qspec(12), kvspec(16), kvspec(20),
                  pl.BlockSpec((1, tq, LANES), lambda b, j, i: (b, i, 0)),
                  pl.BlockSpec((1, nq, N_FGATE, tq), lambda b, j, i: (b, 0, 0, 0)),
                  gnspec],
        out_specs=ospec, out_shape=out_shape, compiler_params=sem, name="fox_attn",
    )(qkv, qkv, qkv, cumc, jnp.transpose(cumr.reshape(bsz, N_FGATE, nq, tq), (0, 2, 1, 3)),
      gn_fox.reshape(N_PAIRS, 1, LANES))
    return o_sb, o_fx


def _outproj_kernel(x_ref, sb_ref, fx_ref, mod_ref, w_ref, o_ref):
    half = sb_ref.shape[-1]
    y = (jnp.dot(sb_ref[0], w_ref[0:half, :], preferred_element_type=F32)
         + jnp.dot(fx_ref[0], w_ref[half:, :], preferred_element_type=F32))
    o_ref[0] = x_ref[0] + mod_ref[0, 2:3, :] * y


def _outproj(x, o_sb, o_fx, mod3, w_out, tm=512):
    bsz, seq, d = x.shape
    half = o_sb.shape[-1]
    return pl.pallas_call(
        _outproj_kernel,
        grid=(bsz, seq // tm),
        in_specs=[pl.BlockSpec((1, tm, d), lambda b, s: (b, s, 0)),
                  pl.BlockSpec((1, tm, half), lambda b, s: (b, s, 0)),
                  pl.BlockSpec((1, tm, half), lambda b, s: (b, s, 0)),
                  pl.BlockSpec((1, N_MOD, d), lambda b, s: (b, 0, 0)),
                  pl.BlockSpec((2 * half, d), lambda b, s: (0, 0))],
        out_specs=pl.BlockSpec((1, tm, d), lambda b, s: (b, s, 0)),
        out_shape=jax.ShapeDtypeStruct((bsz, seq, d), F32),
        compiler_params=_cparams(("parallel", "parallel")),
        name="outproj",
    )(x, o_sb, o_fx, mod3, w_out)


def _peerprep_kernel(x_ref, mod_ref, g_ref, wq_ref, sk_ref,
                     h2_ref, r2_ref, e2_ref, lc_ref, e1_ref,
                     s0_scr, sw_scr, rk_scr, a_scr):
    x = x_ref[0]
    tm = x.shape[0]
    sh = mod_ref[0, 3:4, :]
    sc = mod_ref[0, 4:5, :]
    ms = jnp.mean(x * x, axis=-1, keepdims=True)
    y = x * lax.rsqrt(ms + EPS) * g_ref[...]
    hb = (y * (1.0 + sc) + sh).astype(BF16)
    h2_ref[0] = hb
    qp = jnp.dot(hb, wq_ref[...], preferred_element_type=F32).astype(BF16)
    nhp = 2 * PEER_HEADS
    for ph in range(nhp):
        p, h = divmod(ph, PEER_HEADS)
        col = (h * 2 + p) * N_KEYS
        st = lax.dot_general(sk_ref[ph], qp[:, col:col + N_KEYS], NT_DIMS,
                             preferred_element_type=F32)
        s0_scr[ph] = st
        sw_scr[ph] = st
    rk_scr[...] = jnp.full(rk_scr.shape, float(PEER_TOPK), F32)
    key_id = lax.broadcasted_iota(jnp.int32, (nhp, N_KEYS, tm), 1).astype(F32)

    def topk_round(r, carry):
        sw = sw_scr[...]
        m = jnp.max(sw, axis=1, keepdims=True)
        first = jnp.min(jnp.where(sw == m, key_id, float(N_KEYS)), axis=1, keepdims=True)
        sel = key_id == first
        sw_scr[...] = jnp.where(sel, NEG_INF, sw)
        rk_scr[...] = jnp.where(sel, r.astype(F32), rk_scr[...])
        a_scr[r] = m.reshape(nhp, tm)
        return carry

    lax.fori_loop(0, PEER_TOPK, topk_round, 0)

    a1 = [a_scr[r, 0:PEER_HEADS, :] for r in range(PEER_TOPK)]
    a2 = [a_scr[r, PEER_HEADS:nhp, :] for r in range(PEER_TOPK)]
    cand = [a1[r1] + a2[r2] for (r1, r2) in CAND]
    taken = [jnp.zeros_like(cand[0]) for _ in CAND]
    ncand = len(CAND)
    for _ in range(PEER_TOPK):
        m = cand[0]
        for cnd in cand[1:]:
            m = jnp.maximum(m, cnd)
        first = jnp.full_like(m, float(ncand))
        for idx in range(ncand - 1, -1, -1):
            first = jnp.where(cand[idx] == m, float(idx), first)
        for idx in range(ncand):
            hit = first == float(idx)
            taken[idx] = jnp.where(hit, 1.0, taken[idx])
            cand[idx] = jnp.where(hit, NEG_INF, cand[idx])
    ea1 = [jnp.exp(a1[r] - a1[0]) for r in range(PEER_TOPK)]
    ea2 = [jnp.exp(a2[r] - a2[0]) for r in range(PEER_TOPK)]
    zsum = jnp.zeros_like(cand[0])
    width = [jnp.zeros_like(cand[0]) for _ in range(PEER_TOPK)]
    for idx, (r1, r2) in enumerate(CAND):
        zsum = zsum + taken[idx] * (ea1[r1] * ea2[r2])
        width[r1] = width[r1] + taken[idx]
    inv_z = 1.0 / zsum

    for h in range(PEER_HEADS):
        s1 = s0_scr[h]
        s2 = s0_scr[PEER_HEADS + h]
        rk1 = rk_scr[h]
        lc = jnp.zeros_like(s1)
        for r in range(PEER_TOPK):
            lc = jnp.where(rk1 == float(r), width[r][h:h + 1, :], lc)
        lc_ref[h] = lc
        e1_ref[h] = jnp.exp(s1 - a1[0][h:h + 1, :])
        e2_ref[h] = jnp.exp(s2 - a2[0][h:h + 1, :]) * inv_z[h:h + 1, :]
        r2_ref[h] = rk_scr[PEER_HEADS + h]


def _peerprep(x1, mod3, g, w_query, sub_keys_r, tm=256):
    bsz, seq, d = x1.shape
    t = bsz * seq
    nhp = 2 * PEER_HEADS
    per_b = seq // tm
    dense = jax.ShapeDtypeStruct((PEER_HEADS, N_KEYS, t), F32)
    dspec = pl.BlockSpec((PEER_HEADS, N_KEYS, tm), lambda b, s: (0, 0, b * per_b + s))
    return pl.pallas_call(
        _peerprep_kernel,
        grid=(bsz, per_b),
        in_specs=[pl.BlockSpec((1, tm, d), lambda b, s: (b, s, 0)),
                  pl.BlockSpec((1, N_MOD, d), lambda b, s: (b, 0, 0)),
                  pl.BlockSpec((1, d), lambda b, s: (0, 0)),
                  pl.BlockSpec((d, nhp * N_KEYS), lambda b, s: (0, 0)),
                  pl.BlockSpec((nhp, N_KEYS, N_KEYS), lambda b, s: (0, 0, 0))],
        out_specs=[pl.BlockSpec((1, tm, d), lambda b, s: (b, s, 0)),
                   dspec, dspec, dspec, dspec],
        out_shape=[jax.ShapeDtypeStruct((bsz, seq, d), BF16), dense, dense, dense, dense],
        scratch_shapes=[pltpu.VMEM((nhp, N_KEYS, tm), F32),
                        pltpu.VMEM((nhp, N_KEYS, tm), F32),
                        pltpu.VMEM((nhp, N_KEYS, tm), F32),
                        pltpu.VMEM((PEER_TOPK, nhp, tm), F32)],
        compiler_params=_cparams(("parallel", "parallel")),
        name="peerprep",
    )(x1, mod3, g, w_query, sub_keys_r)


def _peermain_kernel(h2_ref, u_ref, v_ref, r2_ref, e2_ref, lc_ref, e1_ref, x1_ref, modp_ref, gf_ref,
                     o_ref, acc_ref, *, cpb):
    c = pl.program_id(1)

    @pl.when(c == 0)
    def _():
        acc_ref[...] = jnp.zeros_like(acc_ref)

    ht = lax.dot_general(u_ref[...], h2_ref[...], NT_DIMS, preferred_element_type=F32)
    act = 0.5 * ht * (1.0 + lax.erf(ht * INV_SQRT2))
    gates = []
    for cc in range(cpb):
        g = None
        for h in range(PEER_HEADS):
            lrow = lc_ref[cc, h:h + 1, :]
            erow = e1_ref[cc, h:h + 1, :]
            term = jnp.where(r2_ref[h] < lrow, e2_ref[h], 0.0) * erow
            g = term if g is None else g + term
        gates.append(g)
    gate = jnp.concatenate(gates, axis=0) if cpb > 1 else gates[0]
    pt = (gate * act).astype(BF16)
    acc_ref[...] += lax.dot_general(pt, v_ref[...], TN_DIMS, preferred_element_type=F32)

    @pl.when(c == pl.num_programs(1) - 1)
    def _():
        x2 = x1_ref[...] + modp_ref[0, 0:1, :] * acc_ref[...]
        ms = jnp.mean(x2 * x2, axis=-1, keepdims=True)
        y = x2 * lax.rsqrt(ms + EPS) * gf_ref[...]
        o_ref[...] = y * (1.0 + modp_ref[0, 2:3, :]) + modp_ref[0, 1:2, :]


def _peermain(h2, u_bf, v_bf, r2, e2, lc_t, e1_t, x1, modp, gf, seq, tm=512, cpb=2):
    t, d = h2.shape
    per_b = seq // tm
    nc = N_KEYS // cpb
    ce = cpb * N_KEYS
    dspec = pl.BlockSpec((PEER_HEADS, N_KEYS, tm), lambda i, c: (0, 0, i))
    cspec = pl.BlockSpec((cpb, PEER_HEADS, tm), lambda i, c: (c, 0, i))
    return pl.pallas_call(
        functools.partial(_peermain_kernel, cpb=cpb),
        grid=(t // tm, nc),
        in_specs=[pl.BlockSpec((tm, d), lambda i, c: (i, 0)),
                  pl.BlockSpec((ce, d), lambda i, c: (c, 0)),
                  pl.BlockSpec((ce, d), lambda i, c: (c, 0)),
                  dspec, dspec, cspec, cspec,
                  pl.BlockSpec((tm, d), lambda i, c: (i, 0)),
                  pl.BlockSpec((1, 8, d), lambda i, c: (i // per_b, 0, 0)),
                  pl.BlockSpec((1, d), lambda i, c: (0, 0))],
        out_specs=pl.BlockSpec((tm, d), lambda i, c: (i, 0)),
        out_shape=jax.ShapeDtypeStruct((t, d), F32),
        scratch_shapes=[pltpu.VMEM((tm, d), F32)],
        compiler_params=_cparams(("parallel", "arbitrary")),
        name="peermain",
    )(h2, u_bf, v_bf, r2, e2, lc_t, e1_t, x1, modp, gf)


def kernel(x, c, w_ada, b_ada, norm_mix_g, w_in, b_fgate, gn_sb_g, gn_fox_g, w_out, norm_ffn_g,
           w_query, sub_keys, expert_u, expert_v, w_ada_final, b_ada_final, norm_final_g):
    bsz, seq, d = x.shape
    depth = w_ada.shape[0]
    assert depth == 1 and d == D_MODEL
    modf = _ada(c, w_ada_final, b_ada_final)
    l = 0
    mod3 = _ada(c, w_ada[l], b_ada[l]).reshape(bsz, N_MOD, d)

    w_qkv = w_in[l][:, :QKV_COLS].astype(BF16)
    w_f = jnp.pad(w_in[l][:, QKV_COLS:], ((0, 0), (0, LANES - N_FGATE))).astype(BF16)
    b_f = jnp.pad(b_fgate[l], (0, LANES - N_FGATE)).reshape(1, LANES)
    qkv, cumc, cumr = _inproj(x, mod3, norm_mix_g[l].reshape(1, d), w_qkv, w_f, b_f)
    o_sb, o_fx = _attention(qkv, cumc, cumr, gn_sb_g[l], gn_fox_g[l])
    x1 = _outproj(x, o_sb, o_fx, mod3, w_out[l].astype(BF16))

    sk = jnp.transpose(sub_keys[l], (1, 0, 2, 3)).reshape(2 * PEER_HEADS, N_KEYS, N_KEYS).astype(BF16)
    h2, r2, e2, lc, e1 = _peerprep(x1, mod3, norm_ffn_g[l].reshape(1, d), w_query[l].astype(BF16), sk)
    lc_t = jnp.transpose(lc, (1, 0, 2))
    e1_t = jnp.transpose(e1, (1, 0, 2))
    modp = jnp.concatenate([mod3[:, 5:6, :], modf.reshape(bsz, 2, d),
                            jnp.zeros((bsz, 5, d), F32)], axis=1)
    out = _peermain(h2.reshape(bsz * seq, d), expert_u[l].astype(BF16), expert_v[l].astype(BF16),
                    r2, e2, lc_t, e1_t, x1.reshape(bsz * seq, d), modp,
                    norm_final_g.reshape(1, d), seq)
    return out.reshape(bsz, seq, d)
```

```python
import functools
import math

import jax
import jax.numpy as jnp
from jax import lax
from jax.experimental import pallas as pl
from jax.experimental.pallas import tpu as pltpu

F32 = jnp.float32
BF16 = jnp.bfloat16

D_MODEL = 1024
HEAD_DIM = 64
N_PAIRS = 4
LANES = 128
BF16_ROWS = 16
QKV_COLS = 3072
N_FGATE = 8
N_MOD = 6
EPS = 1e-6
PEER_HEADS = 8
N_KEYS = 128
PEER_TOPK = 16
N_EXPERTS = N_KEYS * N_KEYS
INV_SQRT_HD = 1.0 / math.sqrt(HEAD_DIM)
INV_SQRT2 = 1.0 / math.sqrt(2.0)
LOG2E = 1.0 / math.log(2.0)
NEG_INF = float("-inf")

VMEM_LIMIT = 56 * 1024 * 1024

NT_DIMS = (((1,), (1,)), ((), ()))
TN_DIMS = (((0,), (0,)), ((), ()))

CAND = [(r1, r2) for r1 in range(PEER_TOPK) for r2 in range(PEER_TOPK)
        if (r1 + 1) * (r2 + 1) <= PEER_TOPK]


def _cparams(sem, flags=None):
    return pltpu.CompilerParams(dimension_semantics=sem, vmem_limit_bytes=VMEM_LIMIT, flags=flags)


def _softplus(z):
    return jnp.maximum(z, 0.0) + jnp.log(1.0 + jnp.exp2(jnp.abs(z) * (-LOG2E)))


def _split3(x):
    hi = x.astype(BF16)
    r1 = x - hi.astype(F32)
    mid = r1.astype(BF16)
    lo = (r1 - mid.astype(F32)).astype(BF16)
    return hi, mid, lo


def _ada_kernel(c_ref, w_ref, b_ref, o_ref):
    c = c_ref[...]
    ca = c * jax.nn.sigmoid(c)
    o_ref[...] = jnp.dot(ca, w_ref[...], preferred_element_type=F32) + b_ref[...]


def _ada(c, w, b, tn=1024):
    bsz, d = c.shape
    n = w.shape[1]
    return pl.pallas_call(
        _ada_kernel,
        grid=(n // tn,),
        in_specs=[pl.BlockSpec((bsz, d), lambda j: (0, 0)),
                  pl.BlockSpec((d, tn), lambda j: (0, j)),
                  pl.BlockSpec((1, tn), lambda j: (0, j))],
        out_specs=pl.BlockSpec((bsz, tn), lambda j: (0, j)),
        out_shape=jax.ShapeDtypeStruct((bsz, n), F32),
        compiler_params=_cparams(("arbitrary",)),
        name="ada",
    )(c, w, b.reshape(1, n))


def _inproj_kernel(x_ref, mod_ref, g_ref, w_ref, wf_ref, bf_ref, tri_ref,
                   qkv_ref, cumr_ref, carry_ref):
    s = pl.program_id(1)

    @pl.when(s == 0)
    def _():
        carry_ref[...] = jnp.zeros_like(carry_ref)

    x = x_ref[0]
    sh = mod_ref[0, 0:1, :]
    sc = mod_ref[0, 1:2, :]
    ms = jnp.mean(x * x, axis=-1, keepdims=True)
    y = x * lax.rsqrt(ms + EPS) * g_ref[...]
    hb = (y * (1.0 + sc) + sh).astype(BF16)
    qkv_ref[0] = jnp.dot(hb, w_ref[...], preferred_element_type=F32).astype(BF16)
    fl = jnp.dot(hb, wf_ref[...], preferred_element_type=F32) + bf_ref[...]
    lf = -_softplus(-fl)
    tri = tri_ref[...]
    hi, mid, lo = _split3(lf)
    cum = (jnp.dot(tri, hi, preferred_element_type=F32)
           + jnp.dot(tri, mid, preferred_element_type=F32)
           + jnp.dot(tri, lo, preferred_element_type=F32)) + carry_ref[...]
    tm = cum.shape[0]
    carry_ref[...] = cum[tm - 1:tm, :]
    cumr_ref[0] = cum.T[0:N_FGATE, :]


def _inproj(x, mod3, g, w_qkv, w_f, b_f, tm=512):
    bsz, seq, d = x.shape
    ii = lax.broadcasted_iota(jnp.int32, (tm, tm), 0)
    jj = lax.broadcasted_iota(jnp.int32, (tm, tm), 1)
    tri = (jj <= ii).astype(BF16)
    return pl.pallas_call(
        _inproj_kernel,
        grid=(bsz, seq // tm),
        in_specs=[pl.BlockSpec((1, tm, d), lambda b, s: (b, s, 0)),
                  pl.BlockSpec((1, N_MOD, d), lambda b, s: (b, 0, 0)),
                  pl.BlockSpec((1, d), lambda b, s: (0, 0)),
                  pl.BlockSpec((d, QKV_COLS), lambda b, s: (0, 0)),
                  pl.BlockSpec((d, LANES), lambda b, s: (0, 0)),
                  pl.BlockSpec((1, LANES), lambda b, s: (0, 0)),
                  pl.BlockSpec((tm, tm), lambda b, s: (0, 0))],
        out_specs=[pl.BlockSpec((1, tm, QKV_COLS), lambda b, s: (b, s, 0)),
                   pl.BlockSpec((1, N_FGATE, tm), lambda b, s: (b, 0, s))],
        out_shape=[jax.ShapeDtypeStruct((bsz, seq, QKV_COLS), BF16),
                   jax.ShapeDtypeStruct((bsz, N_FGATE, seq), F32)],
        scratch_shapes=[pltpu.VMEM((1, LANES), F32)],
        compiler_params=_cparams(("parallel", "arbitrary")),
        name="inproj",
    )(x, mod3, g, w_qkv, w_f, b_f, tri)


def _head_rmsnorm_store(o_ref, acc_a, acc_b, gn_ref, lane_lo):
    o = jnp.where(lane_lo, acc_a, acc_b)
    sq = o * o
    ss_a = jnp.sum(jnp.where(lane_lo, sq, 0.0), axis=-1, keepdims=True)
    ss_b = jnp.sum(jnp.where(lane_lo, 0.0, sq), axis=-1, keepdims=True)
    ms = jnp.where(lane_lo, ss_a, ss_b) * (1.0 / HEAD_DIM)
    o_ref[0] = (o * lax.rsqrt(ms + EPS) * gn_ref[0]).astype(o_ref.dtype)


def _split_heads(q_ref):
    lane_lo = lax.broadcasted_iota(jnp.int32, (1, LANES), 1) < HEAD_DIM
    q2 = q_ref[0] * INV_SQRT_HD
    zero = jnp.zeros_like(q2)
    return jnp.where(lane_lo, q2, zero), jnp.where(lane_lo, zero, q2), lane_lo


def _sb_kernel(q_ref, k_ref, v_ref, gn_ref, m_ref, o_ref, *, tq, tk):
    i = pl.program_id(2)
    ratio = tq // tk
    qa, qb, lane_lo = _split_heads(q_ref)
    neg_tri = m_ref[...]
    row = lax.broadcasted_iota(jnp.int32, (tq, tk), 0)
    col = lax.broadcasted_iota(jnp.int32, (tq, tk), 1)

    def block(qh, kblk, vblk, acc, carry, off):
        z = lax.dot_general(qh, kblk, NT_DIMS, preferred_element_type=F32)
        sp = _softplus(z)
        if off is not None:
            strict = col + off < row
            sp = jnp.where(strict, sp, 0.0)
        aft = jnp.dot(sp.astype(BF16), neg_tri, preferred_element_type=F32)
        w = jnp.exp((z - sp) + aft + carry)
        if off is not None:
            w = jnp.where(strict, w, 0.0)
        acc = acc + jnp.dot(w.astype(BF16), vblk, preferred_element_type=F32)
        carry = carry + (aft[:, 0:1] - sp[:, 0:1])
        return acc, carry

    def step(blk, state, off):
        start = pl.multiple_of(blk * tk, tk)
        kb = k_ref[0, pl.ds(start, tk), :]
        vb = v_ref[0, pl.ds(start, tk), :]
        acc_a, car_a, acc_b, car_b = state
        acc_a, car_a = block(qa, kb, vb, acc_a, car_a, off)
        acc_b, car_b = block(qb, kb, vb, acc_b, car_b, off)
        return acc_a, car_a, acc_b, car_b

    zacc = jnp.zeros((tq, LANES), F32)
    zcar = jnp.zeros((tq, 1), F32)
    state = (zacc, zcar, zacc, zcar)
    for d in range(ratio - 1, -1, -1):
        state = step(i * ratio + d, state, d * tk)
    nfull = i * ratio

    def body(n, st):
        for d in range(ratio):
            st = step(nfull - 1 - (n * ratio + d), st, None)
        return st

    acc_a, _, acc_b, _ = lax.fori_loop(0, i, body, state)
    _head_rmsnorm_store(o_ref, acc_a, acc_b, gn_ref, lane_lo)


def _fox_kernel(q_ref, k_ref, v_ref, cr_ref, gn_ref, o_ref, *, tq, tk):
    j = pl.program_id(1)
    i = pl.program_id(2)
    ratio = tq // tk
    qa, qb, lane_lo = _split_heads(q_ref)
    row = lax.broadcasted_iota(jnp.int32, (tq, tk), 0)
    col = lax.broadcasted_iota(jnp.int32, (tq, tk), 1)

    def block(qh, kblk, vblk, ck, st, off):
        m, l, acc = st
        z = lax.dot_general(qh, kblk, NT_DIMS, preferred_element_type=F32) - ck
        if off is not None:
            z = jnp.where(col + off <= row, z, NEG_INF)
        m_new = jnp.maximum(m, jnp.max(z, axis=-1, keepdims=True))
        alpha = jnp.exp(m - m_new)
        p = jnp.exp(z - m_new)
        l = alpha * l + jnp.sum(p, axis=-1, keepdims=True)
        acc = alpha * acc + jnp.dot(p.astype(BF16), vblk, preferred_element_type=F32)
        return m_new, l, acc

    def step(blk, state, off):
        start = pl.multiple_of(blk * tk, tk)
        kb = k_ref[0, pl.ds(start, tk), :]
        vb = v_ref[0, pl.ds(start, tk), :]
        ck_a = cr_ref[0, blk, pl.ds(2 * j, 1), :]
        ck_b = cr_ref[0, blk, pl.ds(2 * j + 1, 1), :]
        st_a, st_b = state
        return (block(qa, kb, vb, ck_a, st_a, off), block(qb, kb, vb, ck_b, st_b, off))

    init = (jnp.full((tq, 1), NEG_INF, F32), jnp.zeros((tq, 1), F32), jnp.zeros((tq, LANES), F32))
    state = (init, init)
    for d in range(ratio):
        state = step(i * ratio + d, state, d * tk)

    def body(n, st):
        for d in range(ratio):
            st = step(n * ratio + d, st, None)
        return st

    (_, l_a, acc_a), (_, l_b, acc_b) = lax.fori_loop(0, i, body, state)
    _head_rmsnorm_store(o_ref, acc_a / l_a, acc_b / l_b, gn_ref, lane_lo)


def _attention(qkv, cumr, gn_sb, gn_fox, tq=512, tk=256):
    bsz, seq, _ = qkv.shape
    nq = seq // tq
    nk = seq // tk
    ii = lax.broadcasted_iota(jnp.int32, (tk, tk), 0)
    jj = lax.broadcasted_iota(jnp.int32, (tk, tk), 1)
    neg_tri = jnp.where(ii > jj, -1.0, 0.0).astype(BF16)
    grid = (bsz, N_PAIRS, nq)
    sem = _cparams(("parallel", "parallel", "arbitrary"))

    def qspec(off):
        return pl.BlockSpec((1, tq, LANES), lambda b, j, i: (b, i, off + j))

    def kvspec(off):
        return pl.BlockSpec((1, seq, LANES), lambda b, j, i: (b, 0, off + j))

    gnspec = pl.BlockSpec((1, 1, LANES), lambda b, j, i: (j, 0, 0))
    out_shape = jax.ShapeDtypeStruct((bsz, seq, N_PAIRS * LANES), BF16)
    ospec = pl.BlockSpec((1, tq, LANES), lambda b, j, i: (b, i, j))

    o_sb = pl.pallas_call(
        functools.partial(_sb_kernel, tq=tq, tk=tk),
        grid=grid,
        in_specs=[qspec(0), kvspec(4), kvspec(8), gnspec,
                  pl.BlockSpec((tk, tk), lambda b, j, i: (0, 0))],
        out_specs=ospec, out_shape=out_shape, compiler_params=sem, name="sb_attn",
    )(qkv, qkv, qkv, gn_sb.reshape(N_PAIRS, 1, LANES), neg_tri)

    cum_blocks = jnp.transpose(cumr.reshape(bsz, N_FGATE, nk, tk), (0, 2, 1, 3))
    o_fx = pl.pallas_call(
        functools.partial(_fox_kernel, tq=tq, tk=tk),
        grid=grid,
        in_specs=[qspec(12), kvspec(16), kvspec(20),
                  pl.BlockSpec((1, nk, N_FGATE, tk), lambda b, j, i: (b, 0, 0, 0)),
                  gnspec],
        out_specs=ospec, out_shape=out_shape, compiler_params=sem, name="fox_attn",
    )(qkv, qkv, qkv, cum_blocks, gn_fox.reshape(N_PAIRS, 1, LANES))
    return o_sb, o_fx


def _outproj_kernel(x_ref, sb_ref, fx_ref, mod_ref, w_ref, o_ref):
    half = sb_ref.shape[-1]
    y = (jnp.dot(sb_ref[0], w_ref[0:half, :], preferred_element_type=F32)
         + jnp.dot(fx_ref[0], w_ref[half:, :], preferred_element_type=F32))
    o_ref[0] = x_ref[0] + mod_ref[0, 2:3, :] * y


def _outproj(x, o_sb, o_fx, mod3, w_out, tm=512):
    bsz, seq, d = x.shape
    half = o_sb.shape[-1]
    return pl.pallas_call(
        _outproj_kernel,
        grid=(bsz, seq // tm),
        in_specs=[pl.BlockSpec((1, tm, d), lambda b, s: (b, s, 0)),
                  pl.BlockSpec((1, tm, half), lambda b, s: (b, s, 0)),
                  pl.BlockSpec((1, tm, half), lambda b, s: (b, s, 0)),
                  pl.BlockSpec((1, N_MOD, d), lambda b, s: (b, 0, 0)),
                  pl.BlockSpec((2 * half, d), lambda b, s: (0, 0))],
        out_specs=pl.BlockSpec((1, tm, d), lambda b, s: (b, s, 0)),
        out_shape=jax.ShapeDtypeStruct((bsz, seq, d), F32),
        compiler_params=_cparams(("parallel", "parallel")),
        name="outproj",
    )(x, o_sb, o_fx, mod3, w_out)


def _peerprep_kernel(x_ref, mod_ref, g_ref, wq_ref, sk_ref,
                     h2_ref, r2_ref, e2_ref, lc_ref, e1_ref,
                     s0_scr, sw_scr, rk_scr, a_scr):
    x = x_ref[0]
    tm = x.shape[0]
    sh = mod_ref[0, 3:4, :]
    sc = mod_ref[0, 4:5, :]
    ms = jnp.mean(x * x, axis=-1, keepdims=True)
    y = x * lax.rsqrt(ms + EPS) * g_ref[...]
    hmod = y * (1.0 + sc) + sh
    hb = hmod.astype(BF16)
    h2_ref[...] = hmod.T.astype(BF16)
    qp = jnp.dot(hb, wq_ref[...], preferred_element_type=F32).astype(BF16)
    nhp = 2 * PEER_HEADS
    for ph in range(nhp):
        p, h = divmod(ph, PEER_HEADS)
        col = (h * 2 + p) * N_KEYS
        st = lax.dot_general(sk_ref[ph], qp[:, col:col + N_KEYS], NT_DIMS,
                             preferred_element_type=F32)
        s0_scr[ph] = st
        sw_scr[ph] = st
    not_picked = jnp.full(rk_scr.shape, float(PEER_TOPK), F32)
    rk_scr[...] = not_picked

    def distinct_round(r, carry):
        sw = sw_scr[...]
        m = jnp.max(sw, axis=1, keepdims=True)
        sel = sw == m
        sw_scr[...] = jnp.where(sel, NEG_INF, sw)
        rk_scr[...] = jnp.where(sel, r.astype(F32), rk_scr[...])
        a_scr[r] = m.reshape(nhp, tm)
        return carry

    lax.fori_loop(0, PEER_TOPK, distinct_round, 0)
    picked = jnp.sum(jnp.where(rk_scr[...] < float(PEER_TOPK), 1.0, 0.0), axis=1)
    has_ties = jnp.max(jnp.abs(picked - float(PEER_TOPK))) > 0.0

    @pl.when(has_ties)
    def _():
        key_id = lax.broadcasted_iota(jnp.int32, (nhp, N_KEYS, tm), 1).astype(F32)
        sw_scr[...] = s0_scr[...]
        rk_scr[...] = not_picked

        def ordered_round(r, carry):
            sw = sw_scr[...]
            m = jnp.max(sw, axis=1, keepdims=True)
            first = jnp.min(jnp.where(sw == m, key_id, float(N_KEYS)), axis=1, keepdims=True)
            sel = key_id == first
            sw_scr[...] = jnp.where(sel, NEG_INF, sw)
            rk_scr[...] = jnp.where(sel, r.astype(F32), rk_scr[...])
            a_scr[r] = m.reshape(nhp, tm)
            return carry

        lax.fori_loop(0, PEER_TOPK, ordered_round, 0)

    a1 = [a_scr[r, 0:PEER_HEADS, :] for r in range(PEER_TOPK)]
    a2 = [a_scr[r, PEER_HEADS:nhp, :] for r in range(PEER_TOPK)]
    cand = [a1[r1] + a2[r2] for (r1, r2) in CAND]
    taken = [jnp.zeros_like(cand[0]) for _ in CAND]
    ncand = len(CAND)
    for _ in range(PEER_TOPK):
        m = cand[0]
        for cnd in cand[1:]:
            m = jnp.maximum(m, cnd)
        first = jnp.full_like(m, float(ncand))
        for idx in range(ncand - 1, -1, -1):
            first = jnp.where(cand[idx] == m, float(idx), first)
        for idx in range(ncand):
            hit = first == float(idx)
            taken[idx] = jnp.where(hit, 1.0, taken[idx])
            cand[idx] = jnp.where(hit, NEG_INF, cand[idx])
    ea1 = [jnp.exp(a1[r] - a1[0]) for r in range(PEER_TOPK)]
    ea2 = [jnp.exp(a2[r] - a2[0]) for r in range(PEER_TOPK)]
    zsum = jnp.zeros_like(cand[0])
    width = [jnp.zeros_like(cand[0]) for _ in range(PEER_TOPK)]
    for idx, (r1, r2) in enumerate(CAND):
        zsum = zsum + taken[idx] * (ea1[r1] * ea2[r2])
        width[r1] = width[r1] + taken[idx]
    half_inv_z = 0.5 / zsum

    lcs, e1s = [], []
    tiles = N_KEYS // BF16_ROWS
    for h in range(PEER_HEADS):
        s1 = s0_scr[h]
        s2 = s0_scr[PEER_HEADS + h]
        rk1 = rk_scr[h]
        lc = jnp.zeros_like(s1)
        for r in range(PEER_TOPK):
            lc = jnp.where(rk1 == float(r), width[r][h:h + 1, :], lc)
        lcs.append(lc)
        e1s.append(jnp.exp(s1 - a1[0][h:h + 1, :]))
        e2 = jnp.exp(s2 - a2[0][h:h + 1, :]) * half_inv_z[h:h + 1, :]
        e2_ref[h] = e2.astype(BF16).reshape(tiles, BF16_ROWS, tm)
        r2_ref[h] = rk_scr[PEER_HEADS + h].astype(BF16).reshape(tiles, BF16_ROWS, tm)
    lc_ref[...] = jnp.stack(lcs, axis=1)
    e1_ref[...] = jnp.stack(e1s, axis=1)


def _peerprep(x1, mod3, g, w_query, sub_keys_r, tm=256):
    bsz, seq, d = x1.shape
    t = bsz * seq
    nhp = 2 * PEER_HEADS
    per_b = seq // tm
    tiles = N_KEYS // BF16_ROWS
    packed = jax.ShapeDtypeStruct((PEER_HEADS, tiles, BF16_ROWS, t), BF16)
    pspec = pl.BlockSpec((PEER_HEADS, tiles, BF16_ROWS, tm), lambda b, s: (0, 0, 0, b * per_b + s))
    keyed = jax.ShapeDtypeStruct((N_KEYS, PEER_HEADS, t), F32)
    kspec = pl.BlockSpec((N_KEYS, PEER_HEADS, tm), lambda b, s: (0, 0, b * per_b + s))
    return pl.pallas_call(
        _peerprep_kernel,
        grid=(bsz, per_b),
        in_specs=[pl.BlockSpec((1, tm, d), lambda b, s: (b, s, 0)),
                  pl.BlockSpec((1, N_MOD, d), lambda b, s: (b, 0, 0)),
                  pl.BlockSpec((1, d), lambda b, s: (0, 0)),
                  pl.BlockSpec((d, nhp * N_KEYS), lambda b, s: (0, 0)),
                  pl.BlockSpec((nhp, N_KEYS, N_KEYS), lambda b, s: (0, 0, 0))],
        out_specs=[pl.BlockSpec((d, tm), lambda b, s: (0, b * per_b + s)),
                   pspec, pspec, kspec, kspec],
        out_shape=[jax.ShapeDtypeStruct((d, t), BF16), packed, packed, keyed, keyed],
        scratch_shapes=[pltpu.VMEM((nhp, N_KEYS, tm), F32),
                        pltpu.VMEM((nhp, N_KEYS, tm), F32),
                        pltpu.VMEM((nhp, N_KEYS, tm), F32),
                        pltpu.VMEM((PEER_TOPK, nhp, tm), F32)],
        compiler_params=_cparams(("parallel", "parallel")),
        name="peerprep",
    )(x1, mod3, g, w_query, sub_keys_r)


SET_CHUNKS = 2
SET_ROWS = SET_CHUNKS * N_KEYS
N_SETS = N_KEYS // SET_CHUNKS


def _peer_hidden(u_ref, h2t_ref, ht_ref, cols):
    ht_ref[:, cols] = jnp.dot(u_ref[...], h2t_ref[:, cols], preferred_element_type=F32)


def _peer_weights(ht_ref, p_ref, r2_ref, e2_ref, lc_ref, e1_ref, cols):
    tsub = cols.stop - cols.start
    gates = [None] * SET_CHUNKS
    for h in range(PEER_HEADS):
        r2 = r2_ref[h, :, :, cols]
        e2 = e2_ref[h, :, :, cols]
        for cc in range(SET_CHUNKS):
            lrow = jnp.broadcast_to(lc_ref[cc, h:h + 1, cols], (BF16_ROWS, tsub)).astype(BF16)[None]
            erow = jnp.broadcast_to(e1_ref[cc, h:h + 1, cols], (BF16_ROWS, tsub)).astype(BF16)[None]
            term = jnp.where(r2 < lrow, e2, 0) * erow
            gates[cc] = term if gates[cc] is None else gates[cc] + term
    for cc in range(SET_CHUNKS):
        rows = slice(cc * N_KEYS, (cc + 1) * N_KEYS)
        ht = ht_ref[rows, cols]
        act = (ht * (1.0 + lax.erf(ht * INV_SQRT2))).astype(BF16)
        p_ref[rows, cols] = gates[cc].reshape(N_KEYS, tsub) * act


def _peer_output(p_ref, vt_ref, acc_ref, cols):
    acc_ref[:, cols] += jnp.dot(vt_ref[...], p_ref[:, cols], preferred_element_type=F32)


def _peer_slot(out_args, weight_args, hidden_args, tm, tsub=256):
    for t0 in range(0, tm, tsub):
        cols = slice(t0, t0 + tsub)
        _peer_output(*out_args, cols)
        _peer_weights(*weight_args, cols)
        _peer_hidden(*hidden_args, cols)


def _peermain_kernel(h2t_ref, u0_ref, u1_ref, v0_ref, v1_ref, r2_ref, e2_ref,
                     lca_ref, e1a_ref, lcb_ref, e1b_ref, x1_ref, modp_ref, gf_ref,
                     o_ref, acc_ref, ht0_ref, ht1_ref, p0_ref, p1_ref):
    k = pl.program_id(1)
    tm = acc_ref.shape[1]

    @pl.when(k == 0)
    def _():
        acc_ref[...] = jnp.zeros_like(acc_ref)
        ht1_ref[...] = jnp.zeros_like(ht1_ref)
        p0_ref[...] = jnp.zeros_like(p0_ref)
        p1_ref[...] = jnp.zeros_like(p1_ref)

    _peer_slot((p0_ref, v0_ref, acc_ref), (ht1_ref, p1_ref, r2_ref, e2_ref, lca_ref, e1a_ref),
               (u0_ref, h2t_ref, ht0_ref), tm)
    _peer_slot((p1_ref, v1_ref, acc_ref), (ht0_ref, p0_ref, r2_ref, e2_ref, lcb_ref, e1b_ref),
               (u1_ref, h2t_ref, ht1_ref), tm)

    @pl.when(k == pl.num_programs(1) - 1)
    def _():
        x2 = x1_ref[...] + modp_ref[0, 0:1, :] * acc_ref[...].T
        ms = jnp.mean(x2 * x2, axis=-1, keepdims=True)
        y = x2 * lax.rsqrt(ms + EPS) * gf_ref[...]
        o_ref[...] = y * (1.0 + modp_ref[0, 2:3, :]) + modp_ref[0, 1:2, :]


def _peermain(h2t, u_bf, vt_bf, r2, e2, lc, e1, x1, modp, gf, seq, tm=1024):
    d, t = h2t.shape
    per_b = seq // tm
    nsteps = N_SETS // 2 + 1
    tiles = N_KEYS // BF16_ROWS
    last = N_SETS - 1

    def set_index(k, offset):
        return jnp.clip(2 * k + offset, 0, last)

    def wspec(offset):
        return pl.BlockSpec((SET_ROWS, d), lambda i, k: (set_index(k, offset), 0))

    def vtspec(offset):
        return pl.BlockSpec((d, SET_ROWS), lambda i, k: (0, set_index(k, offset)))

    def cspec(offset):
        return pl.BlockSpec((SET_CHUNKS, PEER_HEADS, tm), lambda i, k: (set_index(k, offset), 0, i))

    pspec = pl.BlockSpec((PEER_HEADS, tiles, BF16_ROWS, tm), lambda i, k: (0, 0, 0, i))
    return pl.pallas_call(
        _peermain_kernel,
        grid=(t // tm, nsteps),
        in_specs=[pl.BlockSpec((d, tm), lambda i, k: (0, i)),
                  wspec(0), wspec(1),
                  vtspec(-2), vtspec(-1),
                  pspec, pspec,
                  cspec(-1), cspec(-1), cspec(0), cspec(0),
                  pl.BlockSpec((tm, d), lambda i, k: (i, 0)),
                  pl.BlockSpec((1, 8, d), lambda i, k: (i // per_b, 0, 0)),
                  pl.BlockSpec((1, d), lambda i, k: (0, 0))],
        out_specs=pl.BlockSpec((tm, d), lambda i, k: (i, 0)),
        out_shape=jax.ShapeDtypeStruct((t, d), F32),
        scratch_shapes=[pltpu.VMEM((d, tm), F32),
                        pltpu.VMEM((SET_ROWS, tm), F32), pltpu.VMEM((SET_ROWS, tm), F32),
                        pltpu.VMEM((SET_ROWS, tm), BF16), pltpu.VMEM((SET_ROWS, tm), BF16)],
        compiler_params=_cparams(("parallel", "arbitrary")),
        name="peermain",
    )(h2t, u_bf, u_bf, vt_bf, vt_bf, r2, e2, lc, e1, lc, e1, x1, modp, gf)


def kernel(x, c, w_ada, b_ada, norm_mix_g, w_in, b_fgate, gn_sb_g, gn_fox_g, w_out, norm_ffn_g,
           w_query, sub_keys, expert_u, expert_v, w_ada_final, b_ada_final, norm_final_g):
    bsz, seq, d = x.shape
    depth = w_ada.shape[0]
    assert depth == 1 and d == D_MODEL
    modf = _ada(c, w_ada_final, b_ada_final)
    l = 0
    mod3 = _ada(c, w_ada[l], b_ada[l]).reshape(bsz, N_MOD, d)

    w_qkv = w_in[l][:, :QKV_COLS].astype(BF16)
    w_f = jnp.pad(w_in[l][:, QKV_COLS:], ((0, 0), (0, LANES - N_FGATE))).astype(BF16)
    b_f = jnp.pad(b_fgate[l], (0, LANES - N_FGATE)).reshape(1, LANES)
    qkv, cumr = _inproj(x, mod3, norm_mix_g[l].reshape(1, d), w_qkv, w_f, b_f)
    o_sb, o_fx = _attention(qkv, cumr, gn_sb_g[l], gn_fox_g[l])
    x1 = _outproj(x, o_sb, o_fx, mod3, w_out[l].astype(BF16))

    sk = jnp.transpose(sub_keys[l], (1, 0, 2, 3)).reshape(2 * PEER_HEADS, N_KEYS, N_KEYS).astype(BF16)
    h2t, r2, e2, lc, e1 = _peerprep(x1, mod3, norm_ffn_g[l].reshape(1, d), w_query[l].astype(BF16), sk)
    modp = jnp.concatenate([mod3[:, 5:6, :], modf.reshape(bsz, 2, d),
                            jnp.zeros((bsz, 5, d), F32)], axis=1)
    out = _peermain(h2t, expert_u[l].astype(BF16), expert_v[l].T.astype(BF16),
                    r2, e2, lc, e1, x1.reshape(bsz * seq, d), modp,
                    norm_final_g.reshape(1, d), seq)
    return out.reshape(bsz, seq, d)
```

```python
import functools
import math

import jax
import jax.numpy as jnp
from jax import lax
from jax.experimental import pallas as pl
from jax.experimental.pallas import tpu as pltpu

F32 = jnp.float32
BF16 = jnp.bfloat16

D_MODEL = 1024
HEAD_DIM = 64
N_PAIRS = 4
LANES = 128
BF16_ROWS = 16
QKV_COLS = 3072
N_FGATE = 8
N_MOD = 6
EPS = 1e-6
PEER_HEADS = 8
N_KEYS = 128
PEER_TOPK = 16
N_EXPERTS = N_KEYS * N_KEYS
INV_SQRT_HD = 1.0 / math.sqrt(HEAD_DIM)
INV_SQRT2 = 1.0 / math.sqrt(2.0)
LOG2E = 1.0 / math.log(2.0)
NEG_INF = float("-inf")

VMEM_LIMIT = 56 * 1024 * 1024

NT_DIMS = (((1,), (1,)), ((), ()))
TN_DIMS = (((0,), (0,)), ((), ()))

CAND = [(r1, r2) for r1 in range(PEER_TOPK) for r2 in range(PEER_TOPK)
        if (r1 + 1) * (r2 + 1) <= PEER_TOPK]


def _cparams(sem, flags=None):
    return pltpu.CompilerParams(dimension_semantics=sem, vmem_limit_bytes=VMEM_LIMIT, flags=flags)


def _softplus(z):
    return jnp.maximum(z, 0.0) + jnp.log(1.0 + jnp.exp2(jnp.abs(z) * (-LOG2E)))


def _split3(x):
    hi = x.astype(BF16)
    r1 = x - hi.astype(F32)
    mid = r1.astype(BF16)
    lo = (r1 - mid.astype(F32)).astype(BF16)
    return hi, mid, lo


def _ada_kernel(c_ref, w_ref, b_ref, o_ref):
    c = c_ref[...]
    ca = c * jax.nn.sigmoid(c)
    o_ref[...] = jnp.dot(ca, w_ref[...], preferred_element_type=F32) + b_ref[...]


def _ada(c, w, b, tn=1024):
    bsz, d = c.shape
    n = w.shape[1]
    return pl.pallas_call(
        _ada_kernel,
        grid=(n // tn,),
        in_specs=[pl.BlockSpec((bsz, d), lambda j: (0, 0)),
                  pl.BlockSpec((d, tn), lambda j: (0, j)),
                  pl.BlockSpec((1, tn), lambda j: (0, j))],
        out_specs=pl.BlockSpec((bsz, tn), lambda j: (0, j)),
        out_shape=jax.ShapeDtypeStruct((bsz, n), F32),
        compiler_params=_cparams(("arbitrary",)),
        name="ada",
    )(c, w, b.reshape(1, n))


def _inproj_kernel(x_ref, mod_ref, g_ref, w_ref, wf_ref, bf_ref, tri_ref,
                   qkv_ref, cumr_ref, carry_ref):
    s = pl.program_id(1)

    @pl.when(s == 0)
    def _():
        carry_ref[...] = jnp.zeros_like(carry_ref)

    x = x_ref[0]
    sh = mod_ref[0, 0:1, :]
    sc = mod_ref[0, 1:2, :]
    ms = jnp.mean(x * x, axis=-1, keepdims=True)
    y = x * lax.rsqrt(ms + EPS) * g_ref[...]
    hb = (y * (1.0 + sc) + sh).astype(BF16)
    qkv_ref[0] = jnp.dot(hb, w_ref[...], preferred_element_type=F32).astype(BF16)
    fl = jnp.dot(hb, wf_ref[...], preferred_element_type=F32) + bf_ref[...]
    lf = -_softplus(-fl)
    tri = tri_ref[...]
    hi, mid, lo = _split3(lf)
    cum = (jnp.dot(tri, hi, preferred_element_type=F32)
           + jnp.dot(tri, mid, preferred_element_type=F32)
           + jnp.dot(tri, lo, preferred_element_type=F32)) + carry_ref[...]
    tm = cum.shape[0]
    carry_ref[...] = cum[tm - 1:tm, :]
    cumr_ref[0] = cum.T[0:N_FGATE, :]


def _inproj(x, mod3, g, w_qkv, w_f, b_f, tm=512):
    bsz, seq, d = x.shape
    ii = lax.broadcasted_iota(jnp.int32, (tm, tm), 0)
    jj = lax.broadcasted_iota(jnp.int32, (tm, tm), 1)
    tri = (jj <= ii).astype(BF16)
    return pl.pallas_call(
        _inproj_kernel,
        grid=(bsz, seq // tm),
        in_specs=[pl.BlockSpec((1, tm, d), lambda b, s: (b, s, 0)),
                  pl.BlockSpec((1, N_MOD, d), lambda b, s: (b, 0, 0)),
                  pl.BlockSpec((1, d), lambda b, s: (0, 0)),
                  pl.BlockSpec((d, QKV_COLS), lambda b, s: (0, 0)),
                  pl.BlockSpec((d, LANES), lambda b, s: (0, 0)),
                  pl.BlockSpec((1, LANES), lambda b, s: (0, 0)),
                  pl.BlockSpec((tm, tm), lambda b, s: (0, 0))],
        out_specs=[pl.BlockSpec((1, tm, QKV_COLS), lambda b, s: (b, s, 0)),
                   pl.BlockSpec((1, N_FGATE, tm), lambda b, s: (b, 0, s))],
        out_shape=[jax.ShapeDtypeStruct((bsz, seq, QKV_COLS), BF16),
                   jax.ShapeDtypeStruct((bsz, N_FGATE, seq), F32)],
        scratch_shapes=[pltpu.VMEM((1, LANES), F32)],
        compiler_params=_cparams(("parallel", "arbitrary")),
        name="inproj",
    )(x, mod3, g, w_qkv, w_f, b_f, tri)


def _head_rmsnorm_store(o_ref, acc_a, acc_b, gn_ref, lane_lo):
    o = jnp.where(lane_lo, acc_a, acc_b)
    sq = o * o
    ss_a = jnp.sum(jnp.where(lane_lo, sq, 0.0), axis=-1, keepdims=True)
    ss_b = jnp.sum(jnp.where(lane_lo, 0.0, sq), axis=-1, keepdims=True)
    ms = jnp.where(lane_lo, ss_a, ss_b) * (1.0 / HEAD_DIM)
    o_ref[0] = (o * lax.rsqrt(ms + EPS) * gn_ref[0]).astype(o_ref.dtype)


def _split_heads(q_ref):
    lane_lo = lax.broadcasted_iota(jnp.int32, (1, LANES), 1) < HEAD_DIM
    q2 = q_ref[0] * INV_SQRT_HD
    zero = jnp.zeros_like(q2)
    return jnp.where(lane_lo, q2, zero), jnp.where(lane_lo, zero, q2), lane_lo


def _sb_kernel(q_ref, k_ref, v_ref, gn_ref, m_ref, o_ref, *, tq, tk):
    i = pl.program_id(2)
    ratio = tq // tk
    qa, qb, lane_lo = _split_heads(q_ref)
    neg_tri = m_ref[...]
    row = lax.broadcasted_iota(jnp.int32, (tq, tk), 0)
    col = lax.broadcasted_iota(jnp.int32, (tq, tk), 1)

    def block(qh, kblk, vblk, acc, carry, off):
        z = lax.dot_general(qh, kblk, NT_DIMS, preferred_element_type=F32)
        sp = _softplus(z)
        if off is not None:
            strict = col + off < row
            sp = jnp.where(strict, sp, 0.0)
        aft = jnp.dot(sp.astype(BF16), neg_tri, preferred_element_type=F32)
        w = jnp.exp((z - sp) + aft + carry)
        if off is not None:
            w = jnp.where(strict, w, 0.0)
        acc = acc + jnp.dot(w.astype(BF16), vblk, preferred_element_type=F32)
        carry = carry + (aft[:, 0:1] - sp[:, 0:1])
        return acc, carry

    def step(blk, state, off):
        start = pl.multiple_of(blk * tk, tk)
        kb = k_ref[0, pl.ds(start, tk), :]
        vb = v_ref[0, pl.ds(start, tk), :]
        acc_a, car_a, acc_b, car_b = state
        acc_a, car_a = block(qa, kb, vb, acc_a, car_a, off)
        acc_b, car_b = block(qb, kb, vb, acc_b, car_b, off)
        return acc_a, car_a, acc_b, car_b

    zacc = jnp.zeros((tq, LANES), F32)
    zcar = jnp.zeros((tq, 1), F32)
    state = (zacc, zcar, zacc, zcar)
    for d in range(ratio - 1, -1, -1):
        state = step(i * ratio + d, state, d * tk)
    nfull = i * ratio

    def body(n, st):
        for d in range(ratio):
            st = step(nfull - 1 - (n * ratio + d), st, None)
        return st

    acc_a, _, acc_b, _ = lax.fori_loop(0, i, body, state)
    _head_rmsnorm_store(o_ref, acc_a, acc_b, gn_ref, lane_lo)


def _fox_kernel(q_ref, k_ref, v_ref, cr_ref, gn_ref, o_ref, *, tq, tk, unroll):
    j = pl.program_id(1)
    i = pl.program_id(2)
    ratio = tq // tk
    qa, qb, lane_lo = _split_heads(q_ref)
    row = lax.broadcasted_iota(jnp.int32, (tq, tk), 0)
    col = lax.broadcasted_iota(jnp.int32, (tq, tk), 1)

    def block(qh, kblk, vblk, ck, st, off):
        m, l, acc = st
        z = lax.dot_general(qh, kblk, NT_DIMS, preferred_element_type=F32) - ck
        if off is not None:
            z = jnp.where(col + off <= row, z, NEG_INF)
        m_new = jnp.maximum(m, jnp.max(z, axis=-1, keepdims=True))
        alpha = jnp.exp(m - m_new)
        p = jnp.exp(z - m_new)
        l = alpha * l + jnp.sum(p, axis=-1, keepdims=True)
        acc = alpha * acc + jnp.dot(p.astype(BF16), vblk, preferred_element_type=F32)
        return m_new, l, acc

    def step(blk, state, off):
        start = pl.multiple_of(blk * tk, tk)
        kb = k_ref[0, pl.ds(start, tk), :]
        vb = v_ref[0, pl.ds(start, tk), :]
        ck_a = cr_ref[0, blk, pl.ds(2 * j, 1), :]
        ck_b = cr_ref[0, blk, pl.ds(2 * j + 1, 1), :]
        st_a, st_b = state
        return (block(qa, kb, vb, ck_a, st_a, off), block(qb, kb, vb, ck_b, st_b, off))

    init = (jnp.full((tq, 1), NEG_INF, F32), jnp.zeros((tq, 1), F32), jnp.zeros((tq, LANES), F32))
    state = (init, init)
    for d in range(ratio):
        state = step(i * ratio + d, state, d * tk)

    nfull = i * ratio

    def body(n, st):
        for d in range(unroll):
            st = step(n * unroll + d, st, None)
        return st

    state = lax.fori_loop(0, nfull // unroll, body, state)
    if ratio % unroll:
        base = (nfull // unroll) * unroll
        state = lax.fori_loop(base, nfull, lambda n, st: step(n, st, None), state)
    (_, l_a, acc_a), (_, l_b, acc_b) = state
    _head_rmsnorm_store(o_ref, acc_a / l_a, acc_b / l_b, gn_ref, lane_lo)


def _attention(qkv, cumr, gn_sb, gn_fox, tq_sb=512, tq_fox=512, tk=256, fox_unroll=2):
    bsz, seq, _ = qkv.shape
    nk = seq // tk
    ii = lax.broadcasted_iota(jnp.int32, (tk, tk), 0)
    jj = lax.broadcasted_iota(jnp.int32, (tk, tk), 1)
    neg_tri = jnp.where(ii > jj, -1.0, 0.0).astype(BF16)
    sem = _cparams(("parallel", "parallel", "arbitrary"))

    def qspec(tq, off):
        return pl.BlockSpec((1, tq, LANES), lambda b, j, i: (b, i, off + j))

    def kvspec(off):
        return pl.BlockSpec((1, seq, LANES), lambda b, j, i: (b, 0, off + j))

    gnspec = pl.BlockSpec((1, 1, LANES), lambda b, j, i: (j, 0, 0))
    out_shape = jax.ShapeDtypeStruct((bsz, seq, N_PAIRS * LANES), BF16)

    o_sb = pl.pallas_call(
        functools.partial(_sb_kernel, tq=tq_sb, tk=tk),
        grid=(bsz, N_PAIRS, seq // tq_sb),
        in_specs=[qspec(tq_sb, 0), kvspec(4), kvspec(8), gnspec,
                  pl.BlockSpec((tk, tk), lambda b, j, i: (0, 0))],
        out_specs=qspec(tq_sb, 0), out_shape=out_shape, compiler_params=sem, name="sb_attn",
    )(qkv, qkv, qkv, gn_sb.reshape(N_PAIRS, 1, LANES), neg_tri)

    cum_blocks = jnp.transpose(cumr.reshape(bsz, N_FGATE, nk, tk), (0, 2, 1, 3))
    o_fx = pl.pallas_call(
        functools.partial(_fox_kernel, tq=tq_fox, tk=tk, unroll=fox_unroll),
        grid=(bsz, N_PAIRS, seq // tq_fox),
        in_specs=[qspec(tq_fox, 12), kvspec(16), kvspec(20),
                  pl.BlockSpec((1, nk, N_FGATE, tk), lambda b, j, i: (b, 0, 0, 0)),
                  gnspec],
        out_specs=qspec(tq_fox, 0), out_shape=out_shape, compiler_params=sem, name="fox_attn",
    )(qkv, qkv, qkv, cum_blocks, gn_fox.reshape(N_PAIRS, 1, LANES))
    return o_sb, o_fx


def _outproj_kernel(x_ref, sb_ref, fx_ref, mod_ref, w_ref, o_ref):
    half = sb_ref.shape[-1]
    y = (jnp.dot(sb_ref[0], w_ref[0:half, :], preferred_element_type=F32)
         + jnp.dot(fx_ref[0], w_ref[half:, :], preferred_element_type=F32))
    o_ref[0] = x_ref[0] + mod_ref[0, 2:3, :] * y


def _outproj(x, o_sb, o_fx, mod3, w_out, tm=512):
    bsz, seq, d = x.shape
    half = o_sb.shape[-1]
    return pl.pallas_call(
        _outproj_kernel,
        grid=(bsz, seq // tm),
        in_specs=[pl.BlockSpec((1, tm, d), lambda b, s: (b, s, 0)),
                  pl.BlockSpec((1, tm, half), lambda b, s: (b, s, 0)),
                  pl.BlockSpec((1, tm, half), lambda b, s: (b, s, 0)),
                  pl.BlockSpec((1, N_MOD, d), lambda b, s: (b, 0, 0)),
                  pl.BlockSpec((2 * half, d), lambda b, s: (0, 0))],
        out_specs=pl.BlockSpec((1, tm, d), lambda b, s: (b, s, 0)),
        out_shape=jax.ShapeDtypeStruct((bsz, seq, d), F32),
        compiler_params=_cparams(("parallel", "parallel")),
        name="outproj",
    )(x, o_sb, o_fx, mod3, w_out)


def _peerprep_kernel(x_ref, mod_ref, g_ref, wq_ref, sk_ref,
                     h2_ref, r2_ref, e2_ref, lc_ref, e1_ref,
                     s0_scr, sw_scr, rk_scr, a_scr, arow_scr):
    x = x_ref[0]
    tm = x.shape[0]
    sh = mod_ref[0, 3:4, :]
    sc = mod_ref[0, 4:5, :]
    ms = jnp.mean(x * x, axis=-1, keepdims=True)
    y = x * lax.rsqrt(ms + EPS) * g_ref[...]
    hmod = y * (1.0 + sc) + sh
    hb = hmod.astype(BF16)
    h2_ref[...] = hmod.T.astype(BF16)
    qp = jnp.dot(hb, wq_ref[...], preferred_element_type=F32).astype(BF16)
    nhp = 2 * PEER_HEADS
    for ph in range(nhp):
        p, h = divmod(ph, PEER_HEADS)
        col = (h * 2 + p) * N_KEYS
        st = lax.dot_general(sk_ref[ph], qp[:, col:col + N_KEYS], NT_DIMS,
                             preferred_element_type=F32)
        s0_scr[ph] = st
    not_picked = jnp.full(rk_scr.shape, float(PEER_TOPK), F32)

    def distinct_round(r, m_prev):
        s0 = s0_scr[...]
        m = jnp.max(jnp.where(s0 < m_prev, s0, NEG_INF), axis=1, keepdims=True)
        a_scr[r] = m.reshape(nhp, tm)
        arow_scr[r] = m
        return m

    m_last = lax.fori_loop(0, PEER_TOPK, distinct_round, jnp.full((nhp, 1, tm), float("inf"), F32))
    picked = jnp.sum(jnp.where(s0_scr[...] >= m_last, 1.0, 0.0), axis=1)
    has_ties = jnp.max(jnp.abs(picked - float(PEER_TOPK))) > 0.0

    @pl.when(jnp.logical_not(has_ties))
    def _():
        s0 = s0_scr[...]
        rk = not_picked
        for r in range(PEER_TOPK - 1, -1, -1):
            rk = jnp.where(s0 >= arow_scr[r], float(r), rk)
        rk_scr[...] = rk

    @pl.when(has_ties)
    def _():
        key_id = lax.broadcasted_iota(jnp.int32, (nhp, N_KEYS, tm), 1).astype(F32)
        sw_scr[...] = s0_scr[...]
        rk_scr[...] = not_picked

        def ordered_round(r, carry):
            sw = sw_scr[...]
            m = jnp.max(sw, axis=1, keepdims=True)
            first = jnp.min(jnp.where(sw == m, key_id, float(N_KEYS)), axis=1, keepdims=True)
            sel = key_id == first
            sw_scr[...] = jnp.where(sel, NEG_INF, sw)
            rk_scr[...] = jnp.where(sel, lax.convert_element_type(r, F32), rk_scr[...])
            a_scr[r] = m.reshape(nhp, tm)
            return carry

        lax.fori_loop(0, PEER_TOPK, ordered_round, 0)

    a1 = [a_scr[r, 0:PEER_HEADS, :] for r in range(PEER_TOPK)]
    a2 = [a_scr[r, PEER_HEADS:nhp, :] for r in range(PEER_TOPK)]
    cand = [a1[r1] + a2[r2] for (r1, r2) in CAND]
    taken = [jnp.zeros_like(cand[0]) for _ in CAND]
    ncand = len(CAND)
    for _ in range(PEER_TOPK):
        m = cand[0]
        for cnd in cand[1:]:
            m = jnp.maximum(m, cnd)
        first = jnp.full_like(m, float(ncand))
        for idx in range(ncand - 1, -1, -1):
            first = jnp.where(cand[idx] == m, float(idx), first)
        for idx in range(ncand):
            hit = first == float(idx)
            taken[idx] = jnp.where(hit, 1.0, taken[idx])
            cand[idx] = jnp.where(hit, NEG_INF, cand[idx])
    ea1 = [jnp.exp(a1[r] - a1[0]) for r in range(PEER_TOPK)]
    ea2 = [jnp.exp(a2[r] - a2[0]) for r in range(PEER_TOPK)]
    zsum = jnp.zeros_like(cand[0])
    width = [jnp.zeros_like(cand[0]) for _ in range(PEER_TOPK)]
    for idx, (r1, r2) in enumerate(CAND):
        zsum = zsum + taken[idx] * (ea1[r1] * ea2[r2])
        width[r1] = width[r1] + taken[idx]
    half_inv_z = 0.5 / zsum

    tiles = N_KEYS // BF16_ROWS
    for h in range(PEER_HEADS):
        s1 = s0_scr[h]
        s2 = s0_scr[PEER_HEADS + h]
        rk1 = rk_scr[h]
        lc = jnp.zeros_like(s1)
        for r in range(PEER_TOPK):
            lc = jnp.where(rk1 == float(r), width[r][h:h + 1, :], lc)
        lc_ref[h] = lc
        e1_ref[h] = jnp.exp(s1 - a1[0][h:h + 1, :])
        e2 = jnp.exp(s2 - a2[0][h:h + 1, :]) * half_inv_z[h:h + 1, :]
        e2_ref[h] = e2.astype(BF16).reshape(tiles, BF16_ROWS, tm)
        r2_ref[h] = rk_scr[PEER_HEADS + h].astype(BF16).reshape(tiles, BF16_ROWS, tm)


def _peerprep(x1, mod3, g, w_query, sub_keys_r, tm=256):
    bsz, seq, d = x1.shape
    t = bsz * seq
    nhp = 2 * PEER_HEADS
    per_b = seq // tm
    tiles = N_KEYS // BF16_ROWS
    packed = jax.ShapeDtypeStruct((PEER_HEADS, tiles, BF16_ROWS, t), BF16)
    pspec = pl.BlockSpec((PEER_HEADS, tiles, BF16_ROWS, tm), lambda b, s: (0, 0, 0, b * per_b + s))
    keyed = jax.ShapeDtypeStruct((PEER_HEADS, N_KEYS, t), F32)
    kspec = pl.BlockSpec((PEER_HEADS, N_KEYS, tm), lambda b, s: (0, 0, b * per_b + s))
    return pl.pallas_call(
        _peerprep_kernel,
        grid=(bsz, per_b),
        in_specs=[pl.BlockSpec((1, tm, d), lambda b, s: (b, s, 0)),
                  pl.BlockSpec((1, N_MOD, d), lambda b, s: (b, 0, 0)),
                  pl.BlockSpec((1, d), lambda b, s: (0, 0)),
                  pl.BlockSpec((d, nhp * N_KEYS), lambda b, s: (0, 0)),
                  pl.BlockSpec((nhp, N_KEYS, N_KEYS), lambda b, s: (0, 0, 0))],
        out_specs=[pl.BlockSpec((d, tm), lambda b, s: (0, b * per_b + s)),
                   pspec, pspec, kspec, kspec],
        out_shape=[jax.ShapeDtypeStruct((d, t), BF16), packed, packed, keyed, keyed],
        scratch_shapes=[pltpu.VMEM((nhp, N_KEYS, tm), F32),
                        pltpu.VMEM((nhp, N_KEYS, tm), F32),
                        pltpu.VMEM((nhp, N_KEYS, tm), F32),
                        pltpu.VMEM((PEER_TOPK, nhp, tm), F32),
                        pltpu.VMEM((PEER_TOPK, nhp, 1, tm), F32)],
        compiler_params=_cparams(("parallel", "parallel")),
        name="peerprep",
    )(x1, mod3, g, w_query, sub_keys_r)


SET_CHUNKS = 2
SET_ROWS = SET_CHUNKS * N_KEYS
N_SETS = N_KEYS // SET_CHUNKS
ROW_GROUP = 8
SETS_PER_GROUP = ROW_GROUP // SET_CHUNKS


def _peer_hidden(u_ref, h2t_ref, ht_ref, cols):
    ht_ref[:, cols] = jnp.dot(u_ref[...], h2t_ref[:, cols], preferred_element_type=F32)


def _row_tile(w_ref, h, row, cols):
    tsub = cols.stop - cols.start
    return jnp.broadcast_to(w_ref[h, pl.ds(row, 1), cols], (BF16_ROWS, tsub)).astype(BF16)[None]


def _peer_weights(ht_ref, p_ref, r2_ref, e2_ref, lc_ref, e1_ref, row0, cols):
    tsub = cols.stop - cols.start
    gates = [None] * SET_CHUNKS
    for h in range(PEER_HEADS):
        r2 = r2_ref[h, :, :, cols]
        e2 = e2_ref[h, :, :, cols]
        for cc in range(SET_CHUNKS):
            term = (jnp.where(r2 < _row_tile(lc_ref, h, row0 + cc, cols), e2, 0)
                    * _row_tile(e1_ref, h, row0 + cc, cols))
            gates[cc] = term if gates[cc] is None else gates[cc] + term
    for cc in range(SET_CHUNKS):
        rows = slice(cc * N_KEYS, (cc + 1) * N_KEYS)
        ht = ht_ref[rows, cols]
        act = (ht * (1.0 + lax.erf(ht * INV_SQRT2))).astype(BF16)
        p_ref[rows, cols] = gates[cc].reshape(N_KEYS, tsub) * act


def _peer_output(p_ref, vt_ref, acc_ref, cols):
    acc_ref[:, cols] += jnp.dot(vt_ref[...], p_ref[:, cols], preferred_element_type=F32)


def _peer_slot(out_args, weight_args, hidden_args, tm, tsub=256):
    for t0 in range(0, tm, tsub):
        cols = slice(t0, t0 + tsub)
        _peer_output(*out_args, cols)
        _peer_weights(*weight_args, cols)
        _peer_hidden(*hidden_args, cols)


def _peermain_kernel(h2t_ref, u0_ref, u1_ref, v0_ref, v1_ref, r2_ref, e2_ref,
                     lca_ref, e1a_ref, lcb_ref, e1b_ref, x1_ref, modp_ref, gf_ref,
                     o_ref, acc_ref, ht0_ref, ht1_ref, p0_ref, p1_ref):
    k = pl.program_id(1)
    tm = acc_ref.shape[1]

    @pl.when(k == 0)
    def _():
        acc_ref[...] = jnp.zeros_like(acc_ref)
        ht1_ref[...] = jnp.zeros_like(ht1_ref)
        p0_ref[...] = jnp.zeros_like(p0_ref)
        p1_ref[...] = jnp.zeros_like(p1_ref)

    def first_row(offset):
        s = jnp.clip(2 * k + offset, 0, N_SETS - 1)
        return (s % SETS_PER_GROUP) * SET_CHUNKS

    _peer_slot((p0_ref, v0_ref, acc_ref),
               (ht1_ref, p1_ref, r2_ref, e2_ref, lca_ref, e1a_ref, first_row(-1)),
               (u0_ref, h2t_ref, ht0_ref), tm)
    _peer_slot((p1_ref, v1_ref, acc_ref),
               (ht0_ref, p0_ref, r2_ref, e2_ref, lcb_ref, e1b_ref, first_row(0)),
               (u1_ref, h2t_ref, ht1_ref), tm)

    @pl.when(k == pl.num_programs(1) - 1)
    def _():
        x2 = x1_ref[...] + modp_ref[0, 0:1, :] * acc_ref[...].T
        ms = jnp.mean(x2 * x2, axis=-1, keepdims=True)
        y = x2 * lax.rsqrt(ms + EPS) * gf_ref[...]
        o_ref[...] = y * (1.0 + modp_ref[0, 2:3, :]) + modp_ref[0, 1:2, :]


def _peermain(h2t, u_bf, vt_bf, r2, e2, lc, e1, x1, modp, gf, seq, tm=1024):
    d, t = h2t.shape
    per_b = seq // tm
    nsteps = N_SETS // 2 + 1
    tiles = N_KEYS // BF16_ROWS
    last = N_SETS - 1

    def set_index(k, offset):
        return jnp.clip(2 * k + offset, 0, last)

    def wspec(offset):
        return pl.BlockSpec((SET_ROWS, d), lambda i, k: (set_index(k, offset), 0))

    def vtspec(offset):
        return pl.BlockSpec((d, SET_ROWS), lambda i, k: (0, set_index(k, offset)))

    def cspec(offset):
        return pl.BlockSpec((PEER_HEADS, ROW_GROUP, tm),
                            lambda i, k: (0, set_index(k, offset) // SETS_PER_GROUP, i))

    pspec = pl.BlockSpec((PEER_HEADS, tiles, BF16_ROWS, tm), lambda i, k: (0, 0, 0, i))
    return pl.pallas_call(
        _peermain_kernel,
        grid=(t // tm, nsteps),
        in_specs=[pl.BlockSpec((d, tm), lambda i, k: (0, i)),
                  wspec(0), wspec(1),
                  vtspec(-2), vtspec(-1),
                  pspec, pspec,
                  cspec(-1), cspec(-1), cspec(0), cspec(0),
                  pl.BlockSpec((tm, d), lambda i, k: (i, 0)),
                  pl.BlockSpec((1, 8, d), lambda i, k: (i // per_b, 0, 0)),
                  pl.BlockSpec((1, d), lambda i, k: (0, 0))],
        out_specs=pl.BlockSpec((tm, d), lambda i, k: (i, 0)),
        out_shape=jax.ShapeDtypeStruct((t, d), F32),
        scratch_shapes=[pltpu.VMEM((d, tm), F32),
                        pltpu.VMEM((SET_ROWS, tm), F32), pltpu.VMEM((SET_ROWS, tm), F32),
                        pltpu.VMEM((SET_ROWS, tm), BF16), pltpu.VMEM((SET_ROWS, tm), BF16)],
        compiler_params=_cparams(("parallel", "arbitrary")),
        name="peermain",
    )(h2t, u_bf, u_bf, vt_bf, vt_bf, r2, e2, lc, e1, lc, e1, x1, modp, gf)


def kernel(x, c, w_ada, b_ada, norm_mix_g, w_in, b_fgate, gn_sb_g, gn_fox_g, w_out, norm_ffn_g,
           w_query, sub_keys, expert_u, expert_v, w_ada_final, b_ada_final, norm_final_g):
    bsz, seq, d = x.shape
    depth = w_ada.shape[0]
    assert depth == 1 and d == D_MODEL
    modf = _ada(c, w_ada_final, b_ada_final)
    l = 0
    mod3 = _ada(c, w_ada[l], b_ada[l]).reshape(bsz, N_MOD, d)

    w_qkv = w_in[l][:, :QKV_COLS].astype(BF16)
    w_f = jnp.pad(w_in[l][:, QKV_COLS:], ((0, 0), (0, LANES - N_FGATE))).astype(BF16)
    b_f = jnp.pad(b_fgate[l], (0, LANES - N_FGATE)).reshape(1, LANES)
    qkv, cumr = _inproj(x, mod3, norm_mix_g[l].reshape(1, d), w_qkv, w_f, b_f)
    o_sb, o_fx = _attention(qkv, cumr, gn_sb_g[l], gn_fox_g[l])
    x1 = _outproj(x, o_sb, o_fx, mod3, w_out[l].astype(BF16))

    sk = jnp.transpose(sub_keys[l], (1, 0, 2, 3)).reshape(2 * PEER_HEADS, N_KEYS, N_KEYS).astype(BF16)
    h2t, r2, e2, lc, e1 = _peerprep(x1, mod3, norm_ffn_g[l].reshape(1, d), w_query[l].astype(BF16), sk)
    modp = jnp.concatenate([mod3[:, 5:6, :], modf.reshape(bsz, 2, d),
                            jnp.zeros((bsz, 5, d), F32)], axis=1)
    out = _peermain(h2t, expert_u[l].astype(BF16), expert_v[l].T.astype(BF16),
                    r2, e2, lc, e1, x1.reshape(bsz * seq, d), modp,
                    norm_final_g.reshape(1, d), seq)
    return out.reshape(bsz, seq, d)
```

```python
import functools
import math

import jax
import jax.numpy as jnp
from jax import lax
from jax.experimental import pallas as pl
from jax.experimental.pallas import tpu as pltpu

F32 = jnp.float32
BF16 = jnp.bfloat16

D_MODEL = 1024
HEAD_DIM = 64
N_PAIRS = 4
LANES = 128
BF16_ROWS = 16
QKV_COLS = 3072
N_FGATE = 8
N_MOD = 6
EPS = 1e-6
PEER_HEADS = 8
N_KEYS = 128
PEER_TOPK = 16
N_EXPERTS = N_KEYS * N_KEYS
INV_SQRT_HD = 1.0 / math.sqrt(HEAD_DIM)
INV_SQRT2 = 1.0 / math.sqrt(2.0)
LOG2E = 1.0 / math.log(2.0)
NEG_INF = float("-inf")

VMEM_LIMIT = 56 * 1024 * 1024

NT_DIMS = (((1,), (1,)), ((), ()))
TN_DIMS = (((0,), (0,)), ((), ()))

CAND = [(r1, r2) for r1 in range(PEER_TOPK) for r2 in range(PEER_TOPK)
        if (r1 + 1) * (r2 + 1) <= PEER_TOPK]


def _cparams(sem, flags=None):
    return pltpu.CompilerParams(dimension_semantics=sem, vmem_limit_bytes=VMEM_LIMIT, flags=flags)


def _softplus(z):
    return jnp.maximum(z, 0.0) + jnp.log(1.0 + jnp.exp2(jnp.abs(z) * (-LOG2E)))


def _split3(x):
    hi = x.astype(BF16)
    r1 = x - hi.astype(F32)
    mid = r1.astype(BF16)
    lo = (r1 - mid.astype(F32)).astype(BF16)
    return hi, mid, lo


def _ada_kernel(c_ref, w_ref, b_ref, o_ref):
    c = c_ref[...]
    ca = c * jax.nn.sigmoid(c)
    o_ref[...] = jnp.dot(ca, w_ref[...], preferred_element_type=F32) + b_ref[...]


def _ada(c, w, b, tn=1024):
    bsz, d = c.shape
    n = w.shape[1]
    return pl.pallas_call(
        _ada_kernel,
        grid=(n // tn,),
        in_specs=[pl.BlockSpec((bsz, d), lambda j: (0, 0)),
                  pl.BlockSpec((d, tn), lambda j: (0, j)),
                  pl.BlockSpec((1, tn), lambda j: (0, j))],
        out_specs=pl.BlockSpec((bsz, tn), lambda j: (0, j)),
        out_shape=jax.ShapeDtypeStruct((bsz, n), F32),
        compiler_params=_cparams(("arbitrary",)),
        name="ada",
    )(c, w, b.reshape(1, n))


def _inproj_kernel(x_ref, mod_ref, g_ref, w_ref, wf_ref, bf_ref, tri_ref,
                   qkv_ref, cumr_ref, carry_ref):
    s = pl.program_id(1)

    @pl.when(s == 0)
    def _():
        carry_ref[...] = jnp.zeros_like(carry_ref)

    x = x_ref[0]
    sh = mod_ref[0, 0:1, :]
    sc = mod_ref[0, 1:2, :]
    ms = jnp.mean(x * x, axis=-1, keepdims=True)
    y = x * lax.rsqrt(ms + EPS) * g_ref[...]
    hb = (y * (1.0 + sc) + sh).astype(BF16)
    qkv_ref[0] = jnp.dot(hb, w_ref[...], preferred_element_type=F32).astype(BF16)
    fl = jnp.dot(hb, wf_ref[...], preferred_element_type=F32) + bf_ref[...]
    lf = -_softplus(-fl)
    tri = tri_ref[...]
    hi, mid, lo = _split3(lf)
    cum = (jnp.dot(tri, hi, preferred_element_type=F32)
           + jnp.dot(tri, mid, preferred_element_type=F32)
           + jnp.dot(tri, lo, preferred_element_type=F32)) + carry_ref[...]
    tm = cum.shape[0]
    carry_ref[...] = cum[tm - 1:tm, :]
    cumr_ref[0] = cum.T[0:N_FGATE, :]


def _inproj(x, mod3, g, w_qkv, w_f, b_f, tm=512):
    bsz, seq, d = x.shape
    ii = lax.broadcasted_iota(jnp.int32, (tm, tm), 0)
    jj = lax.broadcasted_iota(jnp.int32, (tm, tm), 1)
    tri = (jj <= ii).astype(BF16)
    return pl.pallas_call(
        _inproj_kernel,
        grid=(bsz, seq // tm),
        in_specs=[pl.BlockSpec((1, tm, d), lambda b, s: (b, s, 0)),
                  pl.BlockSpec((1, N_MOD, d), lambda b, s: (b, 0, 0)),
                  pl.BlockSpec((1, d), lambda b, s: (0, 0)),
                  pl.BlockSpec((d, QKV_COLS), lambda b, s: (0, 0)),
                  pl.BlockSpec((d, LANES), lambda b, s: (0, 0)),
                  pl.BlockSpec((1, LANES), lambda b, s: (0, 0)),
                  pl.BlockSpec((tm, tm), lambda b, s: (0, 0))],
        out_specs=[pl.BlockSpec((1, tm, QKV_COLS), lambda b, s: (b, s, 0)),
                   pl.BlockSpec((1, N_FGATE, tm), lambda b, s: (b, 0, s))],
        out_shape=[jax.ShapeDtypeStruct((bsz, seq, QKV_COLS), BF16),
                   jax.ShapeDtypeStruct((bsz, N_FGATE, seq), F32)],
        scratch_shapes=[pltpu.VMEM((1, LANES), F32)],
        compiler_params=_cparams(("parallel", "arbitrary")),
        name="inproj",
    )(x, mod3, g, w_qkv, w_f, b_f, tri)


def _head_rmsnorm_store(o_ref, acc_a, acc_b, gn_ref, lane_lo):
    o = jnp.where(lane_lo, acc_a, acc_b)
    sq = o * o
    ss_a = jnp.sum(jnp.where(lane_lo, sq, 0.0), axis=-1, keepdims=True)
    ss_b = jnp.sum(jnp.where(lane_lo, 0.0, sq), axis=-1, keepdims=True)
    ms = jnp.where(lane_lo, ss_a, ss_b) * (1.0 / HEAD_DIM)
    o_ref[0] = (o * lax.rsqrt(ms + EPS) * gn_ref[0]).astype(o_ref.dtype)


def _split_heads(q_ref):
    lane_lo = lax.broadcasted_iota(jnp.int32, (1, LANES), 1) < HEAD_DIM
    q2 = q_ref[0] * INV_SQRT_HD
    zero = jnp.zeros_like(q2)
    return jnp.where(lane_lo, q2, zero), jnp.where(lane_lo, zero, q2), lane_lo


def _sb_kernel(q_ref, k_ref, v_ref, gn_ref, m_ref, o_ref, *, tq, tk):
    i = pl.program_id(2)
    ratio = tq // tk
    qa, qb, lane_lo = _split_heads(q_ref)
    neg_tri = m_ref[...]
    row = lax.broadcasted_iota(jnp.int32, (tq, tk), 0)
    col = lax.broadcasted_iota(jnp.int32, (tq, tk), 1)

    def block(qh, kblk, vblk, acc, carry, off):
        z = lax.dot_general(qh, kblk, NT_DIMS, preferred_element_type=F32)
        sp = _softplus(z)
        if off is not None:
            strict = col + off < row
            sp = jnp.where(strict, sp, 0.0)
        aft = jnp.dot(sp.astype(BF16), neg_tri, preferred_element_type=F32)
        w = jnp.exp((z - sp) + aft + carry)
        if off is not None:
            w = jnp.where(strict, w, 0.0)
        acc = acc + jnp.dot(w.astype(BF16), vblk, preferred_element_type=F32)
        carry = carry + (aft[:, 0:1] - sp[:, 0:1])
        return acc, carry

    def step(blk, state, off):
        start = pl.multiple_of(blk * tk, tk)
        kb = k_ref[0, pl.ds(start, tk), :]
        vb = v_ref[0, pl.ds(start, tk), :]
        acc_a, car_a, acc_b, car_b = state
        acc_a, car_a = block(qa, kb, vb, acc_a, car_a, off)
        acc_b, car_b = block(qb, kb, vb, acc_b, car_b, off)
        return acc_a, car_a, acc_b, car_b

    zacc = jnp.zeros((tq, LANES), F32)
    zcar = jnp.zeros((tq, 1), F32)
    state = (zacc, zcar, zacc, zcar)
    for d in range(ratio - 1, -1, -1):
        state = step(i * ratio + d, state, d * tk)
    nfull = i * ratio

    def body(n, st):
        for d in range(ratio):
            st = step(nfull - 1 - (n * ratio + d), st, None)
        return st

    acc_a, _, acc_b, _ = lax.fori_loop(0, i, body, state)
    _head_rmsnorm_store(o_ref, acc_a, acc_b, gn_ref, lane_lo)


def _fox_kernel(q_ref, k_ref, v_ref, cr_ref, gn_ref, o_ref, *, tq, tk, unroll):
    j = pl.program_id(1)
    i = pl.program_id(2)
    ratio = tq // tk
    qa, qb, lane_lo = _split_heads(q_ref)
    row = lax.broadcasted_iota(jnp.int32, (tq, tk), 0)
    col = lax.broadcasted_iota(jnp.int32, (tq, tk), 1)

    def block(qh, kblk, vblk, ck, st, off):
        m, l, acc = st
        z = lax.dot_general(qh, kblk, NT_DIMS, preferred_element_type=F32) - ck
        if off is not None:
            z = jnp.where(col + off <= row, z, NEG_INF)
        m_new = jnp.maximum(m, jnp.max(z, axis=-1, keepdims=True))
        alpha = jnp.exp(m - m_new)
        p = jnp.exp(z - m_new)
        l = alpha * l + jnp.sum(p, axis=-1, keepdims=True)
        acc = alpha * acc + jnp.dot(p.astype(BF16), vblk, preferred_element_type=F32)
        return m_new, l, acc

    def step(blk, state, off):
        start = pl.multiple_of(blk * tk, tk)
        kb = k_ref[0, pl.ds(start, tk), :]
        vb = v_ref[0, pl.ds(start, tk), :]
        ck_a = cr_ref[0, blk, pl.ds(2 * j, 1), :]
        ck_b = cr_ref[0, blk, pl.ds(2 * j + 1, 1), :]
        st_a, st_b = state
        return (block(qa, kb, vb, ck_a, st_a, off), block(qb, kb, vb, ck_b, st_b, off))

    init = (jnp.full((tq, 1), NEG_INF, F32), jnp.zeros((tq, 1), F32), jnp.zeros((tq, LANES), F32))
    state = (init, init)
    for d in range(ratio):
        state = step(i * ratio + d, state, d * tk)

    nfull = i * ratio

    def body(n, st):
        for d in range(unroll):
            st = step(n * unroll + d, st, None)
        return st

    state = lax.fori_loop(0, nfull // unroll, body, state)
    if ratio % unroll:
        base = (nfull // unroll) * unroll
        state = lax.fori_loop(base, nfull, lambda n, st: step(n, st, None), state)
    (_, l_a, acc_a), (_, l_b, acc_b) = state
    _head_rmsnorm_store(o_ref, acc_a / l_a, acc_b / l_b, gn_ref, lane_lo)


def _attention(qkv, cumr, gn_sb, gn_fox, tq_sb=512, tq_fox=512, tk=256, fox_unroll=2):
    bsz, seq, _ = qkv.shape
    nk = seq // tk
    ii = lax.broadcasted_iota(jnp.int32, (tk, tk), 0)
    jj = lax.broadcasted_iota(jnp.int32, (tk, tk), 1)
    neg_tri = jnp.where(ii > jj, -1.0, 0.0).astype(BF16)
    sem = _cparams(("parallel", "parallel", "arbitrary"))

    def qspec(tq, off):
        return pl.BlockSpec((1, tq, LANES), lambda b, j, i: (b, i, off + j))

    def kvspec(off):
        return pl.BlockSpec((1, seq, LANES), lambda b, j, i: (b, 0, off + j))

    gnspec = pl.BlockSpec((1, 1, LANES), lambda b, j, i: (j, 0, 0))
    out_shape = jax.ShapeDtypeStruct((bsz, seq, N_PAIRS * LANES), BF16)

    o_sb = pl.pallas_call(
        functools.partial(_sb_kernel, tq=tq_sb, tk=tk),
        grid=(bsz, N_PAIRS, seq // tq_sb),
        in_specs=[qspec(tq_sb, 0), kvspec(4), kvspec(8), gnspec,
                  pl.BlockSpec((tk, tk), lambda b, j, i: (0, 0))],
        out_specs=qspec(tq_sb, 0), out_shape=out_shape, compiler_params=sem, name="sb_attn",
    )(qkv, qkv, qkv, gn_sb.reshape(N_PAIRS, 1, LANES), neg_tri)

    cum_blocks = jnp.transpose(cumr.reshape(bsz, N_FGATE, nk, tk), (0, 2, 1, 3))
    o_fx = pl.pallas_call(
        functools.partial(_fox_kernel, tq=tq_fox, tk=tk, unroll=fox_unroll),
        grid=(bsz, N_PAIRS, seq // tq_fox),
        in_specs=[qspec(tq_fox, 12), kvspec(16), kvspec(20),
                  pl.BlockSpec((1, nk, N_FGATE, tk), lambda b, j, i: (b, 0, 0, 0)),
                  gnspec],
        out_specs=qspec(tq_fox, 0), out_shape=out_shape, compiler_params=sem, name="fox_attn",
    )(qkv, qkv, qkv, cum_blocks, gn_fox.reshape(N_PAIRS, 1, LANES))
    return o_sb, o_fx


def _outproj_kernel(x_ref, sb_ref, fx_ref, mod_ref, w_ref, o_ref):
    half = sb_ref.shape[-1]
    y = (jnp.dot(sb_ref[0], w_ref[0:half, :], preferred_element_type=F32)
         + jnp.dot(fx_ref[0], w_ref[half:, :], preferred_element_type=F32))
    o_ref[0] = x_ref[0] + mod_ref[0, 2:3, :] * y


def _outproj(x, o_sb, o_fx, mod3, w_out, tm=512):
    bsz, seq, d = x.shape
    half = o_sb.shape[-1]
    return pl.pallas_call(
        _outproj_kernel,
        grid=(bsz, seq // tm),
        in_specs=[pl.BlockSpec((1, tm, d), lambda b, s: (b, s, 0)),
                  pl.BlockSpec((1, tm, half), lambda b, s: (b, s, 0)),
                  pl.BlockSpec((1, tm, half), lambda b, s: (b, s, 0)),
                  pl.BlockSpec((1, N_MOD, d), lambda b, s: (b, 0, 0)),
                  pl.BlockSpec((2 * half, d), lambda b, s: (0, 0))],
        out_specs=pl.BlockSpec((1, tm, d), lambda b, s: (b, s, 0)),
        out_shape=jax.ShapeDtypeStruct((bsz, seq, d), F32),
        compiler_params=_cparams(("parallel", "parallel")),
        name="outproj",
    )(x, o_sb, o_fx, mod3, w_out)


def _peerprep_kernel(x_ref, mod_ref, g_ref, wq_ref, sk_ref,
                     h2_ref, r2_ref, e2_ref, lc_ref, e1_ref,
                     s0_scr, sw_scr, rk_scr, a_scr, arow_scr, taken_scr):
    x = x_ref[0]
    tm = x.shape[0]
    sh = mod_ref[0, 3:4, :]
    sc = mod_ref[0, 4:5, :]
    ms = jnp.mean(x * x, axis=-1, keepdims=True)
    y = x * lax.rsqrt(ms + EPS) * g_ref[...]
    hmod = y * (1.0 + sc) + sh
    hb = hmod.astype(BF16)
    h2_ref[...] = hmod.T.astype(BF16)
    qp = jnp.dot(hb, wq_ref[...], preferred_element_type=F32).astype(BF16)
    nhp = 2 * PEER_HEADS
    for ph in range(nhp):
        p, h = divmod(ph, PEER_HEADS)
        col = (h * 2 + p) * N_KEYS
        st = lax.dot_general(sk_ref[ph], qp[:, col:col + N_KEYS], NT_DIMS,
                             preferred_element_type=F32)
        s0_scr[ph] = st
    not_picked = jnp.full(rk_scr.shape, float(PEER_TOPK), F32)

    def distinct_round(r, m_prev):
        s0 = s0_scr[...]
        m = jnp.max(jnp.where(s0 < m_prev, s0, NEG_INF), axis=1, keepdims=True)
        a_scr[r] = m.reshape(nhp, tm)
        arow_scr[r] = m
        return m

    m_last = lax.fori_loop(0, PEER_TOPK, distinct_round, jnp.full((nhp, 1, tm), float("inf"), F32))
    picked = jnp.sum(jnp.where(s0_scr[...] >= m_last, 1.0, 0.0), axis=1)
    has_ties = jnp.max(jnp.abs(picked - float(PEER_TOPK))) > 0.0

    @pl.when(has_ties)
    def _():
        key_id = lax.broadcasted_iota(jnp.int32, (nhp, N_KEYS, tm), 1).astype(F32)
        sw_scr[...] = s0_scr[...]
        rk_scr[...] = not_picked

        def ordered_round(r, carry):
            sw = sw_scr[...]
            m = jnp.max(sw, axis=1, keepdims=True)
            first = jnp.min(jnp.where(sw == m, key_id, float(N_KEYS)), axis=1, keepdims=True)
            sel = key_id == first
            sw_scr[...] = jnp.where(sel, NEG_INF, sw)
            rk_scr[...] = jnp.where(sel, lax.convert_element_type(r, F32), rk_scr[...])
            a_scr[r] = m.reshape(nhp, tm)
            return carry

        lax.fori_loop(0, PEER_TOPK, ordered_round, 0)

    a1 = [a_scr[r, 0:PEER_HEADS, :] for r in range(PEER_TOPK)]
    a2 = [a_scr[r, PEER_HEADS:nhp, :] for r in range(PEER_TOPK)]
    cand = [a1[r1] + a2[r2] for (r1, r2) in CAND]
    ncand = len(CAND)
    bound = jnp.full_like(cand[0], float("inf"))
    for _ in range(PEER_TOPK):
        m = None
        for cnd in cand:
            below = jnp.where(cnd < bound, cnd, NEG_INF)
            m = below if m is None else jnp.maximum(m, below)
        bound = m
    n_taken = jnp.zeros_like(cand[0])
    for idx, cnd in enumerate(cand):
        hit = jnp.where(cnd >= bound, 1.0, 0.0)
        taken_scr[idx] = hit
        n_taken = n_taken + hit
    cand_ties = jnp.max(jnp.abs(n_taken - float(PEER_TOPK))) > 0.0

    @pl.when(cand_ties)
    def _():
        left = list(cand)
        taken = [jnp.zeros_like(cand[0]) for _ in CAND]
        for _ in range(PEER_TOPK):
            m = left[0]
            for cnd in left[1:]:
                m = jnp.maximum(m, cnd)
            first = jnp.full_like(m, float(ncand))
            for idx in range(ncand - 1, -1, -1):
                first = jnp.where(left[idx] == m, float(idx), first)
            for idx in range(ncand):
                hit = first == float(idx)
                taken[idx] = jnp.where(hit, 1.0, taken[idx])
                left[idx] = jnp.where(hit, NEG_INF, left[idx])
        for idx in range(ncand):
            taken_scr[idx] = taken[idx]

    taken = [taken_scr[idx] for idx in range(ncand)]
    ea1 = [jnp.exp(a1[r] - a1[0]) for r in range(PEER_TOPK)]
    ea2 = [jnp.exp(a2[r] - a2[0]) for r in range(PEER_TOPK)]
    zsum = jnp.zeros_like(cand[0])
    width = [jnp.zeros_like(cand[0]) for _ in range(PEER_TOPK)]
    for idx, (r1, r2) in enumerate(CAND):
        zsum = zsum + taken[idx] * (ea1[r1] * ea2[r2])
        width[r1] = width[r1] + taken[idx]
    half_inv_z = 0.5 / zsum

    tiles = N_KEYS // BF16_ROWS
    for h in range(PEER_HEADS):
        s1 = s0_scr[h]
        s2 = s0_scr[PEER_HEADS + h]
        e1_ref[h] = jnp.exp(s1 - a1[0][h:h + 1, :])
        e2 = jnp.exp(s2 - a2[0][h:h + 1, :]) * half_inv_z[h:h + 1, :]
        e2_ref[h] = e2.astype(BF16).reshape(tiles, BF16_ROWS, tm)

        @pl.when(jnp.logical_not(has_ties))
        def _():
            lc = jnp.zeros_like(s1)
            rk2 = jnp.full_like(s2, float(PEER_TOPK))
            for r in range(PEER_TOPK - 1, -1, -1):
                lc = jnp.where(s1 >= arow_scr[r, h], width[r][h:h + 1, :], lc)
                rk2 = jnp.where(s2 >= arow_scr[r, PEER_HEADS + h], float(r), rk2)
            lc_ref[h] = lc
            r2_ref[h] = rk2.astype(BF16).reshape(tiles, BF16_ROWS, tm)

        @pl.when(has_ties)
        def _():
            rk1 = rk_scr[h]
            lc = jnp.zeros_like(s1)
            for r in range(PEER_TOPK):
                lc = jnp.where(rk1 == float(r), width[r][h:h + 1, :], lc)
            lc_ref[h] = lc
            r2_ref[h] = rk_scr[PEER_HEADS + h].astype(BF16).reshape(tiles, BF16_ROWS, tm)


def _peerprep(x1, mod3, g, w_query, sub_keys_r, tm=256):
    bsz, seq, d = x1.shape
    t = bsz * seq
    nhp = 2 * PEER_HEADS
    per_b = seq // tm
    tiles = N_KEYS // BF16_ROWS
    packed = jax.ShapeDtypeStruct((PEER_HEADS, tiles, BF16_ROWS, t), BF16)
    pspec = pl.BlockSpec((PEER_HEADS, tiles, BF16_ROWS, tm), lambda b, s: (0, 0, 0, b * per_b + s))
    keyed = jax.ShapeDtypeStruct((PEER_HEADS, N_KEYS, t), F32)
    kspec = pl.BlockSpec((PEER_HEADS, N_KEYS, tm), lambda b, s: (0, 0, b * per_b + s))
    return pl.pallas_call(
        _peerprep_kernel,
        grid=(bsz, per_b),
        in_specs=[pl.BlockSpec((1, tm, d), lambda b, s: (b, s, 0)),
                  pl.BlockSpec((1, N_MOD, d), lambda b, s: (b, 0, 0)),
                  pl.BlockSpec((1, d), lambda b, s: (0, 0)),
                  pl.BlockSpec((d, nhp * N_KEYS), lambda b, s: (0, 0)),
                  pl.BlockSpec((nhp, N_KEYS, N_KEYS), lambda b, s: (0, 0, 0))],
        out_specs=[pl.BlockSpec((d, tm), lambda b, s: (0, b * per_b + s)),
                   pspec, pspec, kspec, kspec],
        out_shape=[jax.ShapeDtypeStruct((d, t), BF16), packed, packed, keyed, keyed],
        scratch_shapes=[pltpu.VMEM((nhp, N_KEYS, tm), F32),
                        pltpu.VMEM((nhp, N_KEYS, tm), F32),
                        pltpu.VMEM((nhp, N_KEYS, tm), F32),
                        pltpu.VMEM((PEER_TOPK, nhp, tm), F32),
                        pltpu.VMEM((PEER_TOPK, nhp, 1, tm), F32),
                        pltpu.VMEM((len(CAND), PEER_HEADS, tm), F32)],
        compiler_params=_cparams(("parallel", "parallel")),
        name="peerprep",
    )(x1, mod3, g, w_query, sub_keys_r)


SET_CHUNKS = 2
SET_ROWS = SET_CHUNKS * N_KEYS
N_SETS = N_KEYS // SET_CHUNKS
ROW_GROUP = 8
SETS_PER_GROUP = ROW_GROUP // SET_CHUNKS


def _peer_hidden(u_ref, h2t_ref, ht_ref, cols):
    ht_ref[:, cols] = jnp.dot(u_ref[...], h2t_ref[:, cols], preferred_element_type=F32)


def _row_tile(w_ref, h, row, cols):
    tsub = cols.stop - cols.start
    return jnp.broadcast_to(w_ref[h, pl.ds(row, 1), cols], (BF16_ROWS, tsub)).astype(BF16)[None]


def _peer_weights(ht_ref, p_ref, r2_ref, e2_ref, lc_ref, e1_ref, row0, cols):
    tsub = cols.stop - cols.start
    for cc in range(SET_CHUNKS):
        gate = None
        for h in range(PEER_HEADS):
            term = (jnp.where(r2_ref[h, :, :, cols] < _row_tile(lc_ref, h, row0 + cc, cols),
                              e2_ref[h, :, :, cols], 0)
                    * _row_tile(e1_ref, h, row0 + cc, cols))
            gate = term if gate is None else gate + term
        rows = slice(cc * N_KEYS, (cc + 1) * N_KEYS)
        ht = ht_ref[rows, cols]
        act = (ht * (1.0 + lax.erf(ht * INV_SQRT2))).astype(BF16)
        p_ref[rows, cols] = gate.reshape(N_KEYS, tsub) * act


def _peer_output(pa_ref, vta_ref, pb_ref, vtb_ref, acc_ref, cols):
    vt = jnp.concatenate([vta_ref[...], vtb_ref[...]], axis=1)
    p = jnp.concatenate([pa_ref[:, cols], pb_ref[:, cols]], axis=0)
    acc_ref[:, cols] += jnp.dot(vt, p, preferred_element_type=F32)


def _peer_slot(out_args, weight_args, hidden_args, tm, tsub=256):
    for t0 in range(0, tm, tsub):
        cols = slice(t0, t0 + tsub)
        if out_args is not None:
            _peer_output(*out_args, cols)
        _peer_weights(*weight_args, cols)
        _peer_hidden(*hidden_args, cols)


def _peermain_kernel(h2t_ref, u0_ref, u1_ref, v0_ref, v1_ref, r2_ref, e2_ref,
                     lca_ref, e1a_ref, lcb_ref, e1b_ref, x1_ref, modp_ref, gf_ref,
                     o_ref, acc_ref, ht0_ref, ht1_ref, p0_ref, p1_ref):
    k = pl.program_id(1)
    tm = acc_ref.shape[1]

    @pl.when(k == 0)
    def _():
        acc_ref[...] = jnp.zeros_like(acc_ref)
        ht1_ref[...] = jnp.zeros_like(ht1_ref)
        p0_ref[...] = jnp.zeros_like(p0_ref)
        p1_ref[...] = jnp.zeros_like(p1_ref)

    def first_row(offset):
        s = jnp.clip(2 * k + offset, 0, N_SETS - 1)
        return (s % SETS_PER_GROUP) * SET_CHUNKS

    _peer_slot(None,
               (ht1_ref, p1_ref, r2_ref, e2_ref, lca_ref, e1a_ref, first_row(-1)),
               (u0_ref, h2t_ref, ht0_ref), tm)
    _peer_slot((p0_ref, v0_ref, p1_ref, v1_ref, acc_ref),
               (ht0_ref, p0_ref, r2_ref, e2_ref, lcb_ref, e1b_ref, first_row(0)),
               (u1_ref, h2t_ref, ht1_ref), tm)

    @pl.when(k == pl.num_programs(1) - 1)
    def _():
        x2 = x1_ref[...] + modp_ref[0, 0:1, :] * acc_ref[...].T
        ms = jnp.mean(x2 * x2, axis=-1, keepdims=True)
        y = x2 * lax.rsqrt(ms + EPS) * gf_ref[...]
        o_ref[...] = y * (1.0 + modp_ref[0, 2:3, :]) + modp_ref[0, 1:2, :]


def _peermain(h2t, u_bf, vt_bf, r2, e2, lc, e1, x1, modp, gf, seq, tm=1024):
    d, t = h2t.shape
    per_b = seq // tm
    nsteps = N_SETS // 2 + 1
    tiles = N_KEYS // BF16_ROWS
    last = N_SETS - 1

    def set_index(k, offset):
        return jnp.clip(2 * k + offset, 0, last)

    def wspec(offset):
        return pl.BlockSpec((SET_ROWS, d), lambda i, k: (set_index(k, offset), 0))

    def vtspec(offset):
        return pl.BlockSpec((d, SET_ROWS), lambda i, k: (0, set_index(k, offset)))

    def cspec(offset):
        return pl.BlockSpec((PEER_HEADS, ROW_GROUP, tm),
                            lambda i, k: (0, set_index(k, offset) // SETS_PER_GROUP, i))

    pspec = pl.BlockSpec((PEER_HEADS, tiles, BF16_ROWS, tm), lambda i, k: (0, 0, 0, i))
    return pl.pallas_call(
        _peermain_kernel,
        grid=(t // tm, nsteps),
        in_specs=[pl.BlockSpec((d, tm), lambda i, k: (0, i)),
                  wspec(0), wspec(1),
                  vtspec(-2), vtspec(-1),
                  pspec, pspec,
                  cspec(-1), cspec(-1), cspec(0), cspec(0),
                  pl.BlockSpec((tm, d), lambda i, k: (i, 0)),
                  pl.BlockSpec((1, 8, d), lambda i, k: (i // per_b, 0, 0)),
                  pl.BlockSpec((1, d), lambda i, k: (0, 0))],
        out_specs=pl.BlockSpec((tm, d), lambda i, k: (i, 0)),
        out_shape=jax.ShapeDtypeStruct((t, d), F32),
        scratch_shapes=[pltpu.VMEM((d, tm), F32),
                        pltpu.VMEM((SET_ROWS, tm), F32), pltpu.VMEM((SET_ROWS, tm), F32),
                        pltpu.VMEM((SET_ROWS, tm), BF16), pltpu.VMEM((SET_ROWS, tm), BF16)],
        compiler_params=_cparams(("parallel", "arbitrary")),
        name="peermain",
    )(h2t, u_bf, u_bf, vt_bf, vt_bf, r2, e2, lc, e1, lc, e1, x1, modp, gf)


def kernel(x, c, w_ada, b_ada, norm_mix_g, w_in, b_fgate, gn_sb_g, gn_fox_g, w_out, norm_ffn_g,
           w_query, sub_keys, expert_u, expert_v, w_ada_final, b_ada_final, norm_final_g):
    bsz, seq, d = x.shape
    depth = w_ada.shape[0]
    assert depth == 1 and d == D_MODEL
    modf = _ada(c, w_ada_final, b_ada_final)
    l = 0
    mod3 = _ada(c, w_ada[l], b_ada[l]).reshape(bsz, N_MOD, d)

    w_qkv = w_in[l][:, :QKV_COLS].astype(BF16)
    w_f = jnp.pad(w_in[l][:, QKV_COLS:], ((0, 0), (0, LANES - N_FGATE))).astype(BF16)
    b_f = jnp.pad(b_fgate[l], (0, LANES - N_FGATE)).reshape(1, LANES)
    qkv, cumr = _inproj(x, mod3, norm_mix_g[l].reshape(1, d), w_qkv, w_f, b_f)
    o_sb, o_fx = _attention(qkv, cumr, gn_sb_g[l], gn_fox_g[l])
    x1 = _outproj(x, o_sb, o_fx, mod3, w_out[l].astype(BF16))

    sk = jnp.transpose(sub_keys[l], (1, 0, 2, 3)).reshape(2 * PEER_HEADS, N_KEYS, N_KEYS).astype(BF16)
    h2t, r2, e2, lc, e1 = _peerprep(x1, mod3, norm_ffn_g[l].reshape(1, d), w_query[l].astype(BF16), sk)
    modp = jnp.concatenate([mod3[:, 5:6, :], modf.reshape(bsz, 2, d),
                            jnp.zeros((bsz, 5, d), F32)], axis=1)
    out = _peermain(h2t, expert_u[l].astype(BF16), expert_v[l].T.astype(BF16),
                    r2, e2, lc, e1, x1.reshape(bsz * seq, d), modp,
                    norm_final_g.reshape(1, d), seq)
    return out.reshape(bsz, seq, d)
```

```python
import functools
import math

import jax
import jax.numpy as jnp
from jax import lax
from jax.experimental import pallas as pl
from jax.experimental.pallas import tpu as pltpu

F32 = jnp.float32
BF16 = jnp.bfloat16

D_MODEL = 1024
HEAD_DIM = 64
N_PAIRS = 4
LANES = 128
BF16_ROWS = 16
QKV_COLS = 3072
N_FGATE = 8
N_MOD = 6
EPS = 1e-6
PEER_HEADS = 8
N_KEYS = 128
PEER_TOPK = 16
N_EXPERTS = N_KEYS * N_KEYS
INV_SQRT_HD = 1.0 / math.sqrt(HEAD_DIM)
INV_SQRT2 = 1.0 / math.sqrt(2.0)
LOG2E = 1.0 / math.log(2.0)
NEG_INF = float("-inf")

VMEM_LIMIT = 56 * 1024 * 1024

NT_DIMS = (((1,), (1,)), ((), ()))
TN_DIMS = (((0,), (0,)), ((), ()))

CAND = [(r1, r2) for r1 in range(PEER_TOPK) for r2 in range(PEER_TOPK)
        if (r1 + 1) * (r2 + 1) <= PEER_TOPK]


def _cparams(sem, flags=None):
    return pltpu.CompilerParams(dimension_semantics=sem, vmem_limit_bytes=VMEM_LIMIT, flags=flags)


def _softplus(z):
    return jnp.maximum(z, 0.0) + jnp.log(1.0 + jnp.exp2(jnp.abs(z) * (-LOG2E)))


def _split3(x):
    hi = x.astype(BF16)
    r1 = x - hi.astype(F32)
    mid = r1.astype(BF16)
    lo = (r1 - mid.astype(F32)).astype(BF16)
    return hi, mid, lo


def _ada_kernel(c_ref, w_ref, b_ref, o_ref):
    c = c_ref[...]
    ca = c * jax.nn.sigmoid(c)
    o_ref[...] = jnp.dot(ca, w_ref[...], preferred_element_type=F32) + b_ref[...]


def _ada(c, w, b, tn=1024):
    bsz, d = c.shape
    n = w.shape[1]
    return pl.pallas_call(
        _ada_kernel,
        grid=(n // tn,),
        in_specs=[pl.BlockSpec((bsz, d), lambda j: (0, 0)),
                  pl.BlockSpec((d, tn), lambda j: (0, j)),
                  pl.BlockSpec((1, tn), lambda j: (0, j))],
        out_specs=pl.BlockSpec((bsz, tn), lambda j: (0, j)),
        out_shape=jax.ShapeDtypeStruct((bsz, n), F32),
        compiler_params=_cparams(("arbitrary",)),
        name="ada",
    )(c, w, b.reshape(1, n))


def _inproj_kernel(x_ref, mod_ref, g_ref, w_ref, wf_ref, bf_ref, tri_ref,
                   qkv_ref, cumr_ref, carry_ref):
    s = pl.program_id(1)

    @pl.when(s == 0)
    def _():
        carry_ref[...] = jnp.zeros_like(carry_ref)

    x = x_ref[0]
    sh = mod_ref[0, 0:1, :]
    sc = mod_ref[0, 1:2, :]
    ms = jnp.mean(x * x, axis=-1, keepdims=True)
    y = x * lax.rsqrt(ms + EPS) * g_ref[...]
    hb = (y * (1.0 + sc) + sh).astype(BF16)
    qkv_ref[0] = jnp.dot(hb, w_ref[...], preferred_element_type=F32).astype(BF16)
    fl = jnp.dot(hb, wf_ref[...], preferred_element_type=F32) + bf_ref[...]
    lf = -_softplus(-fl)
    tri = tri_ref[...]
    hi, mid, lo = _split3(lf)
    cum = (jnp.dot(tri, hi, preferred_element_type=F32)
           + jnp.dot(tri, mid, preferred_element_type=F32)
           + jnp.dot(tri, lo, preferred_element_type=F32)) + carry_ref[...]
    tm = cum.shape[0]
    carry_ref[...] = cum[tm - 1:tm, :]
    cumr_ref[0] = cum.T[0:N_FGATE, :]


def _inproj(x, mod3, g, w_qkv, w_f, b_f, tm=512):
    bsz, seq, d = x.shape
    ii = lax.broadcasted_iota(jnp.int32, (tm, tm), 0)
    jj = lax.broadcasted_iota(jnp.int32, (tm, tm), 1)
    tri = (jj <= ii).astype(BF16)
    return pl.pallas_call(
        _inproj_kernel,
        grid=(bsz, seq // tm),
        in_specs=[pl.BlockSpec((1, tm, d), lambda b, s: (b, s, 0)),
                  pl.BlockSpec((1, N_MOD, d), lambda b, s: (b, 0, 0)),
                  pl.BlockSpec((1, d), lambda b, s: (0, 0)),
                  pl.BlockSpec((d, QKV_COLS), lambda b, s: (0, 0)),
                  pl.BlockSpec((d, LANES), lambda b, s: (0, 0)),
                  pl.BlockSpec((1, LANES), lambda b, s: (0, 0)),
                  pl.BlockSpec((tm, tm), lambda b, s: (0, 0))],
        out_specs=[pl.BlockSpec((1, tm, QKV_COLS), lambda b, s: (b, s, 0)),
                   pl.BlockSpec((1, N_FGATE, tm), lambda b, s: (b, 0, s))],
        out_shape=[jax.ShapeDtypeStruct((bsz, seq, QKV_COLS), BF16),
                   jax.ShapeDtypeStruct((bsz, N_FGATE, seq), F32)],
        scratch_shapes=[pltpu.VMEM((1, LANES), F32)],
        compiler_params=_cparams(("parallel", "arbitrary")),
        name="inproj",
    )(x, mod3, g, w_qkv, w_f, b_f, tri)


def _head_rmsnorm_store(o_ref, acc_a, acc_b, gn_ref, lane_lo):
    o = jnp.where(lane_lo, acc_a, acc_b)
    sq = o * o
    ss_a = jnp.sum(jnp.where(lane_lo, sq, 0.0), axis=-1, keepdims=True)
    ss_b = jnp.sum(jnp.where(lane_lo, 0.0, sq), axis=-1, keepdims=True)
    ms = jnp.where(lane_lo, ss_a, ss_b) * (1.0 / HEAD_DIM)
    o_ref[0] = (o * lax.rsqrt(ms + EPS) * gn_ref[0]).astype(o_ref.dtype)


def _split_heads(q_ref):
    lane_lo = lax.broadcasted_iota(jnp.int32, (1, LANES), 1) < HEAD_DIM
    q2 = q_ref[0] * INV_SQRT_HD
    zero = jnp.zeros_like(q2)
    return jnp.where(lane_lo, q2, zero), jnp.where(lane_lo, zero, q2), lane_lo


def _sb_kernel(q_ref, k_ref, v_ref, gn_ref, m_ref, o_ref, *, tq, tk):
    i = pl.program_id(2)
    ratio = tq // tk
    qa, qb, lane_lo = _split_heads(q_ref)
    neg_tri = m_ref[...]
    row = lax.broadcasted_iota(jnp.int32, (tq, tk), 0)
    col = lax.broadcasted_iota(jnp.int32, (tq, tk), 1)

    def block(qh, kblk, vblk, acc, carry, off):
        z = lax.dot_general(qh, kblk, NT_DIMS, preferred_element_type=F32)
        sp = _softplus(z)
        if off is not None:
            strict = col + off < row
            sp = jnp.where(strict, sp, 0.0)
        aft = jnp.dot(sp.astype(BF16), neg_tri, preferred_element_type=F32)
        w = jnp.exp((z - sp) + aft + carry)
        if off is not None:
            w = jnp.where(strict, w, 0.0)
        acc = acc + jnp.dot(w.astype(BF16), vblk, preferred_element_type=F32)
        carry = carry + (aft[:, 0:1] - sp[:, 0:1])
        return acc, carry

    def step(blk, state, off):
        start = pl.multiple_of(blk * tk, tk)
        kb = k_ref[0, pl.ds(start, tk), :]
        vb = v_ref[0, pl.ds(start, tk), :]
        acc_a, car_a, acc_b, car_b = state
        acc_a, car_a = block(qa, kb, vb, acc_a, car_a, off)
        acc_b, car_b = block(qb, kb, vb, acc_b, car_b, off)
        return acc_a, car_a, acc_b, car_b

    zacc = jnp.zeros((tq, LANES), F32)
    zcar = jnp.zeros((tq, 1), F32)
    state = (zacc, zcar, zacc, zcar)
    for d in range(ratio - 1, -1, -1):
        state = step(i * ratio + d, state, d * tk)
    nfull = i * ratio

    def body(n, st):
        for d in range(ratio):
            st = step(nfull - 1 - (n * ratio + d), st, None)
        return st

    acc_a, _, acc_b, _ = lax.fori_loop(0, i, body, state)
    _head_rmsnorm_store(o_ref, acc_a, acc_b, gn_ref, lane_lo)


def _fox_kernel(q_ref, k_ref, v_ref, cr_ref, gn_ref, o_ref, *, tq, tk, unroll):
    j = pl.program_id(1)
    i = pl.program_id(2)
    ratio = tq // tk
    qa, qb, lane_lo = _split_heads(q_ref)
    row = lax.broadcasted_iota(jnp.int32, (tq, tk), 0)
    col = lax.broadcasted_iota(jnp.int32, (tq, tk), 1)

    def block(qh, kblk, vblk, ck, st, off):
        m, l, acc = st
        z = lax.dot_general(qh, kblk, NT_DIMS, preferred_element_type=F32) - ck
        if off is not None:
            z = jnp.where(col + off <= row, z, NEG_INF)
        m_new = jnp.maximum(m, jnp.max(z, axis=-1, keepdims=True))
        alpha = jnp.exp(m - m_new)
        p = jnp.exp(z - m_new)
        l = alpha * l + jnp.sum(p, axis=-1, keepdims=True)
        acc = alpha * acc + jnp.dot(p.astype(BF16), vblk, preferred_element_type=F32)
        return m_new, l, acc

    def step(blk, state, off):
        start = pl.multiple_of(blk * tk, tk)
        kb = k_ref[0, pl.ds(start, tk), :]
        vb = v_ref[0, pl.ds(start, tk), :]
        ck_a = cr_ref[0, blk, pl.ds(2 * j, 1), :]
        ck_b = cr_ref[0, blk, pl.ds(2 * j + 1, 1), :]
        st_a, st_b = state
        return (block(qa, kb, vb, ck_a, st_a, off), block(qb, kb, vb, ck_b, st_b, off))

    init = (jnp.full((tq, 1), NEG_INF, F32), jnp.zeros((tq, 1), F32), jnp.zeros((tq, LANES), F32))
    state = (init, init)
    for d in range(ratio):
        state = step(i * ratio + d, state, d * tk)

    nfull = i * ratio

    def body(n, st):
        for d in range(unroll):
            st = step(n * unroll + d, st, None)
        return st

    state = lax.fori_loop(0, nfull // unroll, body, state)
    if ratio % unroll:
        base = (nfull // unroll) * unroll
        state = lax.fori_loop(base, nfull, lambda n, st: step(n, st, None), state)
    (_, l_a, acc_a), (_, l_b, acc_b) = state
    _head_rmsnorm_store(o_ref, acc_a / l_a, acc_b / l_b, gn_ref, lane_lo)


def _attention(qkv, cumr, gn_sb, gn_fox, tq_sb=512, tq_fox=512, tk=256, fox_unroll=2):
    bsz, seq, _ = qkv.shape
    nk = seq // tk
    ii = lax.broadcasted_iota(jnp.int32, (tk, tk), 0)
    jj = lax.broadcasted_iota(jnp.int32, (tk, tk), 1)
    neg_tri = jnp.where(ii > jj, -1.0, 0.0).astype(BF16)
    sem = _cparams(("parallel", "parallel", "arbitrary"))

    def qspec(tq, off):
        return pl.BlockSpec((1, tq, LANES), lambda b, j, i: (b, i, off + j))

    def kvspec(off):
        return pl.BlockSpec((1, seq, LANES), lambda b, j, i: (b, 0, off + j))

    gnspec = pl.BlockSpec((1, 1, LANES), lambda b, j, i: (j, 0, 0))
    out_shape = jax.ShapeDtypeStruct((bsz, seq, N_PAIRS * LANES), BF16)

    o_sb = pl.pallas_call(
        functools.partial(_sb_kernel, tq=tq_sb, tk=tk),
        grid=(bsz, N_PAIRS, seq // tq_sb),
        in_specs=[qspec(tq_sb, 0), kvspec(4), kvspec(8), gnspec,
                  pl.BlockSpec((tk, tk), lambda b, j, i: (0, 0))],
        out_specs=qspec(tq_sb, 0), out_shape=out_shape, compiler_params=sem, name="sb_attn",
    )(qkv, qkv, qkv, gn_sb.reshape(N_PAIRS, 1, LANES), neg_tri)

    cum_blocks = jnp.transpose(cumr.reshape(bsz, N_FGATE, nk, tk), (0, 2, 1, 3))
    o_fx = pl.pallas_call(
        functools.partial(_fox_kernel, tq=tq_fox, tk=tk, unroll=fox_unroll),
        grid=(bsz, N_PAIRS, seq // tq_fox),
        in_specs=[qspec(tq_fox, 12), kvspec(16), kvspec(20),
                  pl.BlockSpec((1, nk, N_FGATE, tk), lambda b, j, i: (b, 0, 0, 0)),
                  gnspec],
        out_specs=qspec(tq_fox, 0), out_shape=out_shape, compiler_params=sem, name="fox_attn",
    )(qkv, qkv, qkv, cum_blocks, gn_fox.reshape(N_PAIRS, 1, LANES))
    return o_sb, o_fx


def _outproj_kernel(x_ref, sb_ref, fx_ref, mod_ref, w_ref, o_ref):
    half = sb_ref.shape[-1]
    y = (jnp.dot(sb_ref[0], w_ref[0:half, :], preferred_element_type=F32)
         + jnp.dot(fx_ref[0], w_ref[half:, :], preferred_element_type=F32))
    o_ref[0] = x_ref[0] + mod_ref[0, 2:3, :] * y


def _outproj(x, o_sb, o_fx, mod3, w_out, tm=512):
    bsz, seq, d = x.shape
    half = o_sb.shape[-1]
    return pl.pallas_call(
        _outproj_kernel,
        grid=(bsz, seq // tm),
        in_specs=[pl.BlockSpec((1, tm, d), lambda b, s: (b, s, 0)),
                  pl.BlockSpec((1, tm, half), lambda b, s: (b, s, 0)),
                  pl.BlockSpec((1, tm, half), lambda b, s: (b, s, 0)),
                  pl.BlockSpec((1, N_MOD, d), lambda b, s: (b, 0, 0)),
                  pl.BlockSpec((2 * half, d), lambda b, s: (0, 0))],
        out_specs=pl.BlockSpec((1, tm, d), lambda b, s: (b, s, 0)),
        out_shape=jax.ShapeDtypeStruct((bsz, seq, d), F32),
        compiler_params=_cparams(("parallel", "parallel")),
        name="outproj",
    )(x, o_sb, o_fx, mod3, w_out)


def _peerprep_kernel(x_ref, mod_ref, g_ref, wq_ref, sk_ref,
                     h2_ref, r2_ref, e2_ref, lc_ref, e1_ref,
                     s0_scr, sw_scr, rk_scr, a_scr, arow_scr, taken_scr):
    x = x_ref[0]
    tm = x.shape[0]
    sh = mod_ref[0, 3:4, :]
    sc = mod_ref[0, 4:5, :]
    ms = jnp.mean(x * x, axis=-1, keepdims=True)
    y = x * lax.rsqrt(ms + EPS) * g_ref[...]
    hmod = y * (1.0 + sc) + sh
    hb = hmod.astype(BF16)
    h2_ref[...] = hmod.T.astype(BF16)
    qp = jnp.dot(hb, wq_ref[...], preferred_element_type=F32).astype(BF16)
    nhp = 2 * PEER_HEADS
    for ph in range(nhp):
        p, h = divmod(ph, PEER_HEADS)
        col = (h * 2 + p) * N_KEYS
        st = lax.dot_general(sk_ref[ph], qp[:, col:col + N_KEYS], NT_DIMS,
                             preferred_element_type=F32)
        s0_scr[ph] = st
    not_picked = jnp.full(rk_scr.shape, float(PEER_TOPK), F32)

    def distinct_round(r, m_prev):
        s0 = s0_scr[...]
        m = jnp.max(jnp.where(s0 < m_prev, s0, NEG_INF), axis=1, keepdims=True)
        a_scr[r] = m.reshape(nhp, tm)
        arow_scr[r] = m
        return m

    m_last = lax.fori_loop(0, PEER_TOPK, distinct_round, jnp.full((nhp, 1, tm), float("inf"), F32))
    picked = jnp.sum(jnp.where(s0_scr[...] >= m_last, 1.0, 0.0), axis=1)
    has_ties = jnp.max(jnp.abs(picked - float(PEER_TOPK))) > 0.0

    @pl.when(has_ties)
    def _():
        key_id = lax.broadcasted_iota(jnp.int32, (nhp, N_KEYS, tm), 1).astype(F32)
        sw_scr[...] = s0_scr[...]
        rk_scr[...] = not_picked

        def ordered_round(r, carry):
            sw = sw_scr[...]
            m = jnp.max(sw, axis=1, keepdims=True)
            first = jnp.min(jnp.where(sw == m, key_id, float(N_KEYS)), axis=1, keepdims=True)
            sel = key_id == first
            sw_scr[...] = jnp.where(sel, NEG_INF, sw)
            rk_scr[...] = jnp.where(sel, lax.convert_element_type(r, F32), rk_scr[...])
            a_scr[r] = m.reshape(nhp, tm)
            return carry

        lax.fori_loop(0, PEER_TOPK, ordered_round, 0)

    a1 = [a_scr[r, 0:PEER_HEADS, :] for r in range(PEER_TOPK)]
    a2 = [a_scr[r, PEER_HEADS:nhp, :] for r in range(PEER_TOPK)]
    cand = [a1[r1] + a2[r2] for (r1, r2) in CAND]
    ncand = len(CAND)
    bound = jnp.full_like(cand[0], float("inf"))
    for _ in range(PEER_TOPK):
        m = None
        for cnd in cand:
            below = jnp.where(cnd < bound, cnd, NEG_INF)
            m = below if m is None else jnp.maximum(m, below)
        bound = m
    n_taken = jnp.zeros_like(cand[0])
    for idx, cnd in enumerate(cand):
        hit = jnp.where(cnd >= bound, 1.0, 0.0)
        taken_scr[idx] = hit
        n_taken = n_taken + hit
    cand_ties = jnp.max(jnp.abs(n_taken - float(PEER_TOPK))) > 0.0

    @pl.when(cand_ties)
    def _():
        left = list(cand)
        taken = [jnp.zeros_like(cand[0]) for _ in CAND]
        for _ in range(PEER_TOPK):
            m = left[0]
            for cnd in left[1:]:
                m = jnp.maximum(m, cnd)
            first = jnp.full_like(m, float(ncand))
            for idx in range(ncand - 1, -1, -1):
                first = jnp.where(left[idx] == m, float(idx), first)
            for idx in range(ncand):
                hit = first == float(idx)
                taken[idx] = jnp.where(hit, 1.0, taken[idx])
                left[idx] = jnp.where(hit, NEG_INF, left[idx])
        for idx in range(ncand):
            taken_scr[idx] = taken[idx]

    taken = [taken_scr[idx] for idx in range(ncand)]
    ea1 = [jnp.exp(a1[r] - a1[0]) for r in range(PEER_TOPK)]
    ea2 = [jnp.exp(a2[r] - a2[0]) for r in range(PEER_TOPK)]
    zsum = jnp.zeros_like(cand[0])
    width = [jnp.zeros_like(cand[0]) for _ in range(PEER_TOPK)]
    for idx, (r1, r2) in enumerate(CAND):
        zsum = zsum + taken[idx] * (ea1[r1] * ea2[r2])
        width[r1] = width[r1] + taken[idx]
    half_inv_z = 0.5 / zsum

    tiles = N_KEYS // BF16_ROWS
    for h in range(PEER_HEADS):
        s1 = s0_scr[h]
        s2 = s0_scr[PEER_HEADS + h]
        e1_ref[h] = jnp.exp(s1 - a1[0][h:h + 1, :])
        e2 = jnp.exp(s2 - a2[0][h:h + 1, :]) * half_inv_z[h:h + 1, :]
        e2_ref[h] = e2.astype(BF16).reshape(tiles, BF16_ROWS, tm)

        @pl.when(jnp.logical_not(has_ties))
        def _():
            lc = jnp.zeros_like(s1)
            rk2 = jnp.full_like(s2, float(PEER_TOPK))
            for r in range(PEER_TOPK - 1, -1, -1):
                lc = jnp.where(s1 >= arow_scr[r, h], width[r][h:h + 1, :], lc)
                rk2 = jnp.where(s2 >= arow_scr[r, PEER_HEADS + h], float(r), rk2)
            lc_ref[h] = lc
            r2_ref[h] = rk2.astype(BF16).reshape(tiles, BF16_ROWS, tm)

        @pl.when(has_ties)
        def _():
            rk1 = rk_scr[h]
            lc = jnp.zeros_like(s1)
            for r in range(PEER_TOPK):
                lc = jnp.where(rk1 == float(r), width[r][h:h + 1, :], lc)
            lc_ref[h] = lc
            r2_ref[h] = rk_scr[PEER_HEADS + h].astype(BF16).reshape(tiles, BF16_ROWS, tm)


def _peerprep(x1, mod3, g, w_query, sub_keys_r, tm=256):
    bsz, seq, d = x1.shape
    t = bsz * seq
    nhp = 2 * PEER_HEADS
    per_b = seq // tm
    tiles = N_KEYS // BF16_ROWS
    packed = jax.ShapeDtypeStruct((PEER_HEADS, tiles, BF16_ROWS, t), BF16)
    pspec = pl.BlockSpec((PEER_HEADS, tiles, BF16_ROWS, tm), lambda b, s: (0, 0, 0, b * per_b + s))
    keyed = jax.ShapeDtypeStruct((PEER_HEADS, N_KEYS, t), F32)
    kspec = pl.BlockSpec((PEER_HEADS, N_KEYS, tm), lambda b, s: (0, 0, b * per_b + s))
    return pl.pallas_call(
        _peerprep_kernel,
        grid=(bsz, per_b),
        in_specs=[pl.BlockSpec((1, tm, d), lambda b, s: (b, s, 0)),
                  pl.BlockSpec((1, N_MOD, d), lambda b, s: (b, 0, 0)),
                  pl.BlockSpec((1, d), lambda b, s: (0, 0)),
                  pl.BlockSpec((d, nhp * N_KEYS), lambda b, s: (0, 0)),
                  pl.BlockSpec((nhp, N_KEYS, N_KEYS), lambda b, s: (0, 0, 0))],
        out_specs=[pl.BlockSpec((d, tm), lambda b, s: (0, b * per_b + s)),
                   pspec, pspec, kspec, kspec],
        out_shape=[jax.ShapeDtypeStruct((d, t), BF16), packed, packed, keyed, keyed],
        scratch_shapes=[pltpu.VMEM((nhp, N_KEYS, tm), F32),
                        pltpu.VMEM((nhp, N_KEYS, tm), F32),
                        pltpu.VMEM((nhp, N_KEYS, tm), F32),
                        pltpu.VMEM((PEER_TOPK, nhp, tm), F32),
                        pltpu.VMEM((PEER_TOPK, nhp, 1, tm), F32),
                        pltpu.VMEM((len(CAND), PEER_HEADS, tm), F32)],
        compiler_params=_cparams(("parallel", "parallel")),
        name="peerprep",
    )(x1, mod3, g, w_query, sub_keys_r)


SET_CHUNKS = 2
SET_ROWS = SET_CHUNKS * N_KEYS
N_SETS = N_KEYS // SET_CHUNKS
ROW_GROUP = 8
SETS_PER_GROUP = ROW_GROUP // SET_CHUNKS


def _peer_hidden(u_ref, h2t_ref, ht_ref, cols):
    ht_ref[:, cols] = jnp.dot(u_ref[...], h2t_ref[:, cols], preferred_element_type=F32)


def _row_tile(w_ref, h, row, cols):
    tsub = cols.stop - cols.start
    return jnp.broadcast_to(w_ref[h, pl.ds(row, 1), cols], (BF16_ROWS, tsub)).astype(BF16)[None]


def _peer_weights(ht_ref, p_ref, r2_ref, e2_ref, lc_ref, e1_ref, row0, cols):
    tsub = cols.stop - cols.start
    for cc in range(SET_CHUNKS):
        gate = None
        for h in range(PEER_HEADS):
            term = (jnp.where(r2_ref[h, :, :, cols] < _row_tile(lc_ref, h, row0 + cc, cols),
                              e2_ref[h, :, :, cols], 0)
                    * _row_tile(e1_ref, h, row0 + cc, cols))
            gate = term if gate is None else gate + term
        rows = slice(cc * N_KEYS, (cc + 1) * N_KEYS)
        ht = ht_ref[rows, cols]
        act = (ht * (1.0 + lax.erf(ht * INV_SQRT2))).astype(BF16)
        p_ref[rows, cols] = gate.reshape(N_KEYS, tsub) * act


def _peer_output(pa_ref, vta_ref, pb_ref, vtb_ref, acc_ref, cols):
    vt = jnp.concatenate([vta_ref[...], vtb_ref[...]], axis=1)
    p = jnp.concatenate([pa_ref[:, cols], pb_ref[:, cols]], axis=0)
    acc_ref[:, cols] += jnp.dot(vt, p, preferred_element_type=F32)


def _peer_slot(out_args, weight_args, hidden_args, tm, tsub=512):
    for t0 in range(0, tm, tsub):
        cols = slice(t0, t0 + tsub)
        if out_args is not None:
            _peer_output(*out_args, cols)
        _peer_weights(*weight_args, cols)
        _peer_hidden(*hidden_args, cols)


def _peermain_kernel(h2t_ref, u0_ref, u1_ref, v0_ref, v1_ref, r2_ref, e2_ref,
                     lca_ref, e1a_ref, lcb_ref, e1b_ref, x1_ref, modp_ref, gf_ref,
                     o_ref, acc_ref, ht0_ref, ht1_ref, p0_ref, p1_ref):
    k = pl.program_id(1)
    tm = acc_ref.shape[1]

    @pl.when(k == 0)
    def _():
        acc_ref[...] = jnp.zeros_like(acc_ref)
        ht1_ref[...] = jnp.zeros_like(ht1_ref)
        p0_ref[...] = jnp.zeros_like(p0_ref)
        p1_ref[...] = jnp.zeros_like(p1_ref)

    def first_row(offset):
        s = jnp.clip(2 * k + offset, 0, N_SETS - 1)
        return (s % SETS_PER_GROUP) * SET_CHUNKS

    _peer_slot(None,
               (ht1_ref, p1_ref, r2_ref, e2_ref, lca_ref, e1a_ref, first_row(-1)),
               (u0_ref, h2t_ref, ht0_ref), tm)
    _peer_slot((p0_ref, v0_ref, p1_ref, v1_ref, acc_ref),
               (ht0_ref, p0_ref, r2_ref, e2_ref, lcb_ref, e1b_ref, first_row(0)),
               (u1_ref, h2t_ref, ht1_ref), tm)

    @pl.when(k == pl.num_programs(1) - 1)
    def _():
        x2 = x1_ref[...] + modp_ref[0, 0:1, :] * acc_ref[...].T
        ms = jnp.mean(x2 * x2, axis=-1, keepdims=True)
        y = x2 * lax.rsqrt(ms + EPS) * gf_ref[...]
        o_ref[...] = y * (1.0 + modp_ref[0, 2:3, :]) + modp_ref[0, 1:2, :]


def _peermain(h2t, u_bf, vt_bf, r2, e2, lc, e1, x1, modp, gf, seq, tm=1024):
    d, t = h2t.shape
    per_b = seq // tm
    nsteps = N_SETS // 2 + 1
    tiles = N_KEYS // BF16_ROWS
    last = N_SETS - 1

    def set_index(k, offset):
        return jnp.clip(2 * k + offset, 0, last)

    def wspec(offset):
        return pl.BlockSpec((SET_ROWS, d), lambda i, k: (set_index(k, offset), 0))

    def vtspec(offset):
        return pl.BlockSpec((d, SET_ROWS), lambda i, k: (0, set_index(k, offset)))

    def cspec(offset):
        return pl.BlockSpec((PEER_HEADS, ROW_GROUP, tm),
                            lambda i, k: (0, set_index(k, offset) // SETS_PER_GROUP, i))

    pspec = pl.BlockSpec((PEER_HEADS, tiles, BF16_ROWS, tm), lambda i, k: (0, 0, 0, i))
    return pl.pallas_call(
        _peermain_kernel,
        grid=(t // tm, nsteps),
        in_specs=[pl.BlockSpec((d, tm), lambda i, k: (0, i)),
                  wspec(0), wspec(1),
                  vtspec(-2), vtspec(-1),
                  pspec, pspec,
                  cspec(-1), cspec(-1), cspec(0), cspec(0),
                  pl.BlockSpec((tm, d), lambda i, k: (i, 0)),
                  pl.BlockSpec((1, 8, d), lambda i, k: (i // per_b, 0, 0)),
                  pl.BlockSpec((1, d), lambda i, k: (0, 0))],
        out_specs=pl.BlockSpec((tm, d), lambda i, k: (i, 0)),
        out_shape=jax.ShapeDtypeStruct((t, d), F32),
        scratch_shapes=[pltpu.VMEM((d, tm), F32),
                        pltpu.VMEM((SET_ROWS, tm), F32), pltpu.VMEM((SET_ROWS, tm), F32),
                        pltpu.VMEM((SET_ROWS, tm), BF16), pltpu.VMEM((SET_ROWS, tm), BF16)],
        compiler_params=_cparams(("parallel", "arbitrary")),
        name="peermain",
    )(h2t, u_bf, u_bf, vt_bf, vt_bf, r2, e2, lc, e1, lc, e1, x1, modp, gf)


def kernel(x, c, w_ada, b_ada, norm_mix_g, w_in, b_fgate, gn_sb_g, gn_fox_g, w_out, norm_ffn_g,
           w_query, sub_keys, expert_u, expert_v, w_ada_final, b_ada_final, norm_final_g):
    bsz, seq, d = x.shape
    depth = w_ada.shape[0]
    assert depth == 1 and d == D_MODEL
    modf = _ada(c, w_ada_final, b_ada_final)
    l = 0
    mod3 = _ada(c, w_ada[l], b_ada[l]).reshape(bsz, N_MOD, d)

    w_qkv = w_in[l][:, :QKV_COLS].astype(BF16)
    w_f = jnp.pad(w_in[l][:, QKV_COLS:], ((0, 0), (0, LANES - N_FGATE))).astype(BF16)
    b_f = jnp.pad(b_fgate[l], (0, LANES - N_FGATE)).reshape(1, LANES)
    qkv, cumr = _inproj(x, mod3, norm_mix_g[l].reshape(1, d), w_qkv, w_f, b_f)
    o_sb, o_fx = _attention(qkv, cumr, gn_sb_g[l], gn_fox_g[l])
    x1 = _outproj(x, o_sb, o_fx, mod3, w_out[l].astype(BF16))

    sk = jnp.transpose(sub_keys[l], (1, 0, 2, 3)).reshape(2 * PEER_HEADS, N_KEYS, N_KEYS).astype(BF16)
    h2t, r2, e2, lc, e1 = _peerprep(x1, mod3, norm_ffn_g[l].reshape(1, d), w_query[l].astype(BF16), sk)
    modp = jnp.concatenate([mod3[:, 5:6, :], modf.reshape(bsz, 2, d),
                            jnp.zeros((bsz, 5, d), F32)], axis=1)
    out = _peermain(h2t, expert_u[l].astype(BF16), expert_v[l].T.astype(BF16),
                    r2, e2, lc, e1, x1.reshape(bsz * seq, d), modp,
                    norm_final_g.reshape(1, d), seq)
    return out.reshape(bsz, seq, d)
```
